```python
import jax
import jax.numpy as jnp
from jax import lax
import numpy as np

D_MODEL = 2048
BATCH = 2
SEQ = 4096
DEPTH = 2
DEC_BATCH = 128
DEC_SEQ = 1
PAST_LEN = 2048
PAGE_SIZE = 128

N_HEADS = 16
KV_HEADS = 8
HEAD_DIM = 128
Q_PER_KV = N_HEADS // KV_HEADS
MOBA_BLOCK = 256
MOBA_TOPK = 3
ATTN_Q_GROUP = 16
RET_HEADS = 8
RET_DK = D_MODEL // RET_HEADS
RET_DV = 2 * D_MODEL // RET_HEADS
RET_CHUNK = 128
ROT_BASE = 10000.0
N_EXPERTS = 64
TOP_K = 6
D_EXPERT = D_MODEL // 4
D_SHARED = D_MODEL // 4
ROUTED_SCALE = 2.5
MOE_BLOCK = 128
LN_EPS = 1e-5
DEEPNORM_ALPHA = (2.0 * DEPTH) ** 0.25
DEEPNORM_BETA = (8.0 * DEPTH) ** -0.25
N_MOD = 6

ATTN_Q_W = N_HEADS * HEAD_DIM
ATTN_KV_W = KV_HEADS * HEAD_DIM
RET_QK_W = RET_HEADS * RET_DK
RET_V_W = RET_HEADS * RET_DV
IN_SPLITS = (ATTN_Q_W, ATTN_KV_W, ATTN_KV_W, RET_QK_W, RET_QK_W, RET_V_W, RET_V_W, D_MODEL, D_MODEL)
D_IN = ATTN_Q_W + 2 * ATTN_KV_W + 2 * RET_QK_W + 2 * RET_V_W + 2 * D_MODEL

kernel_name = 'moba_retnet_moe_hybrid_step'


def layer_norm(x, g, b):
    xf = x.astype(jnp.float32)
    mu = jnp.mean(xf, axis=-1, keepdims=True)
    var = jnp.mean(jnp.square(xf - mu), axis=-1, keepdims=True)
    y = (xf - mu) * lax.rsqrt(var + LN_EPS) * g.astype(jnp.float32) + b.astype(jnp.float32)
    return y.astype(x.dtype)


def swiglu(x, w_gate, w_up, w_down):
    return (jax.nn.silu(x @ w_gate) * (x @ w_up)) @ w_down


def rotate(x, pos):
    half = x.shape[-1] // 2
    theta = ROT_BASE ** -jnp.linspace(0.0, 1.0, half, dtype=jnp.float32)
    ang = pos.astype(jnp.float32)[:, None] * theta[None, :]
    cos = jnp.cos(ang)[None, :, None, :]
    sin = jnp.sin(ang)[None, :, None, :]
    xf = x.astype(jnp.float32)
    x1, x2 = xf[..., :half], xf[..., half:]
    return jnp.concatenate([x1 * cos - x2 * sin, x2 * cos + x1 * sin], axis=-1)


def moba_blocks(k, v):
    bsz, t = k.shape[:2]
    nb = -(-t // MOBA_BLOCK)
    pad = nb * MOBA_BLOCK - t
    if pad:
        k = jnp.pad(k, ((0, 0), (0, pad), (0, 0), (0, 0)))
        v = jnp.pad(v, ((0, 0), (0, pad), (0, 0), (0, 0)))
    kb = k.reshape(bsz, nb, MOBA_BLOCK, KV_HEADS, HEAD_DIM)
    vb = v.reshape(bsz, nb, MOBA_BLOCK, KV_HEADS, HEAD_DIM)
    kmean = jnp.mean(kb.astype(jnp.float32), axis=2)
    return kb, vb, kmean


def moba_attend(q, q_pos, kb, vb, kmean):
    bsz, lq = q.shape[:2]
    nb = kb.shape[1]
    qg = q.reshape(bsz, lq, KV_HEADS, Q_PER_KV, HEAD_DIM).transpose(0, 2, 3, 1, 4)
    own = q_pos // MOBA_BLOCK
    gate = jnp.einsum('bgrqd,bngd->bgrqn', qg.astype(jnp.float32), kmean)
    gate = jnp.where(jnp.arange(nb)[None, :] < own[:, None], gate, -jnp.inf)
    if nb < MOBA_TOPK:
        gate = jnp.pad(gate, ((0, 0), (0, 0), (0, 0), (0, 0), (0, MOBA_TOPK - nb)), constant_values=-jnp.inf)
    _, top = lax.top_k(gate, MOBA_TOPK)
    top = jnp.minimum(top, nb - 1)
    sel = jnp.concatenate([top, jnp.broadcast_to(own[:, None], top.shape[:-1] + (1,))], axis=-1)
    slot_ok = jnp.concatenate([jnp.arange(MOBA_TOPK)[None, :] < own[:, None],
                               jnp.ones((lq, 1), dtype=bool)], axis=-1)
    kpos = sel[..., None] * MOBA_BLOCK + jnp.arange(MOBA_BLOCK)
    mask = slot_ok[:, :, None] & (kpos <= q_pos[:, None, None])
    bi = jnp.arange(bsz)[:, None, None, None, None]
    gi = jnp.arange(KV_HEADS)[None, :, None, None, None]
    ks = kb[bi, sel, :, gi]
    vs = vb[bi, sel, :, gi]
    s = jnp.einsum('bgrqd,bgrqnjd->bgrqnj', qg, ks).astype(jnp.float32) * (HEAD_DIM ** -0.5)
    s = jnp.where(mask, s, -jnp.inf)
    p = jax.nn.softmax(s.reshape(s.shape[:4] + (-1,)), axis=-1).reshape(s.shape)
    o = jnp.einsum('bgrqnj,bgrqnjd->bgrqd', p.astype(vs.dtype), vs)
    return o.transpose(0, 3, 1, 2, 4).reshape(bsz, lq, ATTN_Q_W)


def prompt_attend(q, k, v):
    bsz, s_len = q.shape[:2]
    kb, vb, kmean = moba_blocks(k, v)
    n_grp = s_len // ATTN_Q_GROUP
    qs = q.reshape(bsz, n_grp, ATTN_Q_GROUP, N_HEADS, HEAD_DIM).swapaxes(0, 1)
    pos = jnp.arange(s_len, dtype=jnp.int32).reshape(n_grp, ATTN_Q_GROUP)
    o = lax.map(lambda a: moba_attend(a[0], a[1], kb, vb, kmean), (qs, pos))
    return o.swapaxes(0, 1).reshape(bsz, s_len, ATTN_Q_W)


def make_sample_attend(cache_k_l, cache_v_l, page_table):
    def attend(q, k, v):
        bsz, lq = q.shape[:2]
        past = page_table.shape[1] * cache_k_l.shape[1]
        k_past = cache_k_l[page_table].reshape(bsz, past, KV_HEADS, HEAD_DIM)
        v_past = cache_v_l[page_table].reshape(bsz, past, KV_HEADS, HEAD_DIM)
        kb, vb, kmean = moba_blocks(jnp.concatenate([k_past, k.astype(k_past.dtype)], axis=1),
                                    jnp.concatenate([v_past, v.astype(v_past.dtype)], axis=1))
        q_pos = past + jnp.arange(lq, dtype=jnp.int32)
        return moba_attend(q.astype(kb.dtype), q_pos, kb, vb, kmean)
    return attend


def retention_log_decay():
    return jnp.log1p(-jnp.exp2(-5.0 - jnp.arange(RET_HEADS, dtype=jnp.float32)))


def retention_chunk(state, q, k, v):
    c = q.shape[1]
    lg = retention_log_decay()
    i = jnp.arange(c, dtype=jnp.float32)
    diff = i[:, None] - i[None, :]
    causal = diff >= 0.0
    dmask = jnp.where(causal[None], jnp.exp(jnp.where(causal, diff, 0.0)[None] * lg[:, None, None]), 0.0)
    scores = jnp.einsum('bihd,bjhd->bhij', q, k) * dmask[None]
    o = jnp.einsum('bhij,bjhv->bihv', scores, v)
    o = o + jnp.einsum('bihd,bhdv->bihv', q, state) * jnp.exp((i[:, None] + 1.0) * lg[None, :])[None, :, :, None]
    k_dec = k * jnp.exp((c - 1.0 - i)[:, None] * lg[None, :])[None, :, :, None]
    new_state = state * jnp.exp(c * lg)[None, :, None, None] + jnp.einsum('bjhd,bjhv->bhdv', k_dec, v)
    return new_state, o


def retention_scan(state, q, k, v):
    bsz, l_len = q.shape[:2]
    c = min(RET_CHUNK, l_len)
    n = l_len // c

    def chunks(t):
        return t.reshape((bsz, n, c) + t.shape[2:]).swapaxes(0, 1)

    state, o = lax.scan(lambda s, xs: retention_chunk(s, xs[0], xs[1], xs[2]), state, (chunks(q), chunks(k), chunks(v)))
    return state, o.swapaxes(0, 1).reshape(bsz, l_len, RET_HEADS, RET_DV)


def routed_experts(h, top_idx, top_w, w_gate, w_up, w_down):
    m, d = h.shape
    a = m * TOP_K
    n_blk = -(-(a + N_EXPERTS * (MOE_BLOCK - 1)) // MOE_BLOCK)
    flat_e = top_idx.reshape(a)
    order = jnp.argsort(flat_e)
    e_sorted = flat_e[order]
    counts = jnp.zeros((N_EXPERTS,), jnp.int32).at[flat_e].add(1)
    padded = (counts + MOE_BLOCK - 1) // MOE_BLOCK * MOE_BLOCK
    start = jnp.cumsum(counts) - counts
    ends = jnp.cumsum(padded)
    pstart = ends - padded
    dest = pstart[e_sorted] + jnp.arange(a, dtype=jnp.int32) - start[e_sorted]
    tok = order // TOP_K
    xs = jnp.zeros((n_blk * MOE_BLOCK, d), h.dtype).at[dest].set(h[tok])
    blk_e = jnp.minimum(jnp.searchsorted(ends, jnp.arange(n_blk, dtype=jnp.int32) * MOE_BLOCK, side='right'),
                        N_EXPERTS - 1)

    def expert_block(args):
        xb, e = args
        return swiglu(xb, w_gate[e], w_up[e], w_down[e])

    ys = lax.map(expert_block, (xs.reshape(n_blk, MOE_BLOCK, d), blk_e)).reshape(n_blk * MOE_BLOCK, d)
    contrib = ys[dest].astype(jnp.float32) * top_w.reshape(a)[order][:, None]
    return jax.ops.segment_sum(contrib, tok, num_segments=m).astype(h.dtype)


def moe_ffn(h, w_router, router_bias, w_exp_gate, w_exp_up, w_exp_down, w_sh_gate, w_sh_up, w_sh_down):
    aff = jax.nn.sigmoid(h.astype(jnp.float32) @ w_router.astype(jnp.float32))
    _, top_idx = lax.top_k(aff + router_bias.astype(jnp.float32), TOP_K)
    top_aff = jnp.take_along_axis(aff, top_idx, axis=-1)
    top_w = top_aff / jnp.sum(top_aff, axis=-1, keepdims=True) * ROUTED_SCALE
    shared = swiglu(h, w_sh_gate, w_sh_up, w_sh_down)
    return shared + routed_experts(h, top_idx, top_w, w_exp_gate, w_exp_up, w_exp_down)


def trunk_layer(x, c, pos, attend, ret_state, w_ada_l, b_ada_l, w_in_l, w_attn_br_l, ret_norm_g_l, w_ret_br_l,
                w_o_l, ln_g_l, ln_b_l, w_router_l, router_bias_l, w_exp_gate_l, w_exp_up_l, w_exp_down_l,
                w_sh_gate_l, w_sh_up_l, w_sh_down_l):
    bsz, l_len, _ = x.shape
    mod = jax.nn.silu(c) @ w_ada_l + b_ada_l
    shift1, scale1, gate1, shift2, scale2, gate2 = jnp.split(mod[:, None, :], N_MOD, axis=-1)
    h = x * (1.0 + scale1) + shift1
    q_a, k_a, v_a, q_r, k_r, v_r, g_r, br_a, br_r = jnp.split(h @ w_in_l, np.cumsum(IN_SPLITS)[:-1].tolist(), axis=-1)
    k_a = k_a.reshape(bsz, l_len, KV_HEADS, HEAD_DIM)
    v_a = v_a.reshape(bsz, l_len, KV_HEADS, HEAD_DIM)
    o_a = attend(q_a.reshape(bsz, l_len, N_HEADS, HEAD_DIM), k_a, v_a)
    qr = rotate(q_r.reshape(bsz, l_len, RET_HEADS, RET_DK), pos)
    kr = rotate(k_r.reshape(bsz, l_len, RET_HEADS, RET_DK), pos) * (RET_DK ** -0.5)
    vr = v_r.reshape(bsz, l_len, RET_HEADS, RET_DV).astype(jnp.float32)
    new_state, o_r = retention_scan(ret_state, qr, kr, vr)
    mu = jnp.mean(o_r, axis=-1, keepdims=True)
    var = jnp.mean(jnp.square(o_r - mu), axis=-1, keepdims=True)
    o_r = ((o_r - mu) * lax.rsqrt(var + LN_EPS)).reshape(bsz, l_len, RET_V_W)
    o_r = o_r * ret_norm_g_l.astype(jnp.float32) * jax.nn.silu(g_r.astype(jnp.float32))
    mixed = (jax.nn.sigmoid(br_a) * (o_a.astype(x.dtype) @ w_attn_br_l)
             + jax.nn.sigmoid(br_r) * (o_r.astype(x.dtype) @ w_ret_br_l))
    x = layer_norm(DEEPNORM_ALPHA * x + gate1 * (mixed @ w_o_l), ln_g_l[0], ln_b_l[0])
    h = x * (1.0 + scale2) + shift2
    ffn = moe_ffn(h.reshape(bsz * l_len, D_MODEL), w_router_l, router_bias_l, w_exp_gate_l, w_exp_up_l,
                  w_exp_down_l, w_sh_gate_l, w_sh_up_l, w_sh_down_l).reshape(bsz, l_len, D_MODEL)
    x = layer_norm(DEEPNORM_ALPHA * x + gate2 * ffn, ln_g_l[1], ln_b_l[1])
    return x, k_a, v_a, new_state


def setup_inputs(seed: int = 0) -> dict:
    key = jax.random.key(seed)
    ks = jax.random.split(key, 26)
    f32 = jnp.float32
    n_pages = PAST_LEN // PAGE_SIZE
    n_used = DEC_BATCH * n_pages
    n_phys = n_used + (n_used + 3) // 4

    def nrm(k, shape, scale):
        return jax.random.normal(k, shape, f32) * scale

    page_table = jax.random.permutation(ks[5], n_phys)[:n_used].reshape(DEC_BATCH, n_pages).astype(jnp.int32)
    return {
        'x_prompt': nrm(ks[0], (BATCH, SEQ, D_MODEL), 1.0),
        'x_sample': nrm(ks[1], (DEC_BATCH, DEC_SEQ, D_MODEL), 1.0),
        'cache_k': nrm(ks[2], (DEPTH, n_phys, PAGE_SIZE, KV_HEADS, HEAD_DIM), 1.0),
        'cache_v': nrm(ks[3], (DEPTH, n_phys, PAGE_SIZE, KV_HEADS, HEAD_DIM), 1.0),
        'state_ret': nrm(ks[4], (DEPTH, DEC_BATCH, RET_HEADS, RET_DK, RET_DV), 1.0),
        'page_table': page_table,
        'c_prompt': nrm(ks[6], (BATCH, D_MODEL), 1.0),
        'c_sample': nrm(ks[7], (DEC_BATCH, D_MODEL), 1.0),
        'w_ada': nrm(ks[8], (DEPTH, D_MODEL, N_MOD * D_MODEL), 0.5 * D_MODEL ** -0.5),
        'b_ada': nrm(ks[9], (DEPTH, N_MOD * D_MODEL), 0.02),
        'w_in': nrm(ks[10], (DEPTH, D_MODEL, D_IN), D_MODEL ** -0.5),
        'w_attn_br': nrm(ks[11], (DEPTH, ATTN_Q_W, D_MODEL), ATTN_Q_W ** -0.5),
        'ret_norm_g': 1.0 + nrm(ks[12], (DEPTH, RET_V_W), 0.02),
        'w_ret_br': nrm(ks[13], (DEPTH, RET_V_W, D_MODEL), RET_V_W ** -0.5),
        'w_o': nrm(ks[14], (DEPTH, D_MODEL, D_MODEL), DEEPNORM_BETA * D_MODEL ** -0.5),
        'ln_g': 1.0 + nrm(ks[15], (DEPTH, 2, D_MODEL), 0.02),
        'ln_b': nrm(ks[16], (DEPTH, 2, D_MODEL), 0.02),
        'w_router': nrm(ks[17], (DEPTH, D_MODEL, N_EXPERTS), D_MODEL ** -0.5),
        'router_bias': nrm(ks[18], (DEPTH, N_EXPERTS), 0.01),
        'w_exp_gate': nrm(ks[19], (DEPTH, N_EXPERTS, D_MODEL, D_EXPERT), D_MODEL ** -0.5),
        'w_exp_up': nrm(ks[20], (DEPTH, N_EXPERTS, D_MODEL, D_EXPERT), D_MODEL ** -0.5),
        'w_exp_down': nrm(ks[21], (DEPTH, N_EXPERTS, D_EXPERT, D_MODEL), DEEPNORM_BETA * D_EXPERT ** -0.5),
        'w_sh_gate': nrm(ks[22], (DEPTH, D_MODEL, D_SHARED), D_MODEL ** -0.5),
        'w_sh_up': nrm(ks[23], (DEPTH, D_MODEL, D_SHARED), D_MODEL ** -0.5),
        'w_sh_down': nrm(ks[24], (DEPTH, D_SHARED, D_MODEL), DEEPNORM_BETA * D_SHARED ** -0.5),
    }


def reference(x_prompt, x_sample, cache_k, cache_v, state_ret, page_table, c_prompt, c_sample, w_ada, b_ada,
              w_in, w_attn_br, ret_norm_g, w_ret_br, w_o, ln_g, ln_b, w_router, router_bias, w_exp_gate,
              w_exp_up, w_exp_down, w_sh_gate, w_sh_up, w_sh_down):
    bsz, s_len = x_prompt.shape[:2]
    l_dec = x_sample.shape[1]
    past = page_table.shape[1] * cache_k.shape[2]
    pos_p = jnp.arange(s_len, dtype=jnp.int32)
    pos_s = past + jnp.arange(l_dec, dtype=jnp.int32)
    xp, xs = x_prompt, x_sample
    kp, vp, sp, kd, vd, sd = [], [], [], [], [], []
    for l in range(DEPTH):
        lw = (w_ada[l], b_ada[l], w_in[l], w_attn_br[l], ret_norm_g[l], w_ret_br[l], w_o[l], ln_g[l], ln_b[l],
              w_router[l], router_bias[l], w_exp_gate[l], w_exp_up[l], w_exp_down[l], w_sh_gate[l], w_sh_up[l],
              w_sh_down[l])
        s0 = jnp.zeros((bsz, RET_HEADS, RET_DK, RET_DV), jnp.float32)
        xp, k_new, v_new, s_new = trunk_layer(xp, c_prompt, pos_p, prompt_attend, s0, *lw)
        kp.append(k_new)
        vp.append(v_new)
        sp.append(s_new.astype(state_ret.dtype))
        attend_s = make_sample_attend(cache_k[l], cache_v[l], page_table)
        xs, k_new, v_new, s_new = trunk_layer(xs, c_sample, pos_s, attend_s, state_ret[l].astype(jnp.float32), *lw)
        kd.append(k_new)
        vd.append(v_new)
        sd.append(s_new.astype(state_ret.dtype))
    return (xp, xs, jnp.stack(kp), jnp.stack(vp), jnp.stack(sp), jnp.stack(kd), jnp.stack(vd), jnp.stack(sd))
```

```python
import functools

import jax
import jax.numpy as jnp
import numpy as np
from jax import lax
from jax.experimental import pallas as pl
from jax.experimental.pallas import tpu as pltpu

F32 = jnp.float32
BF16 = jnp.bfloat16

D_MODEL = 2048
N_HEADS = 16
KV_HEADS = 8
HEAD_DIM = 128
Q_PER_KV = N_HEADS // KV_HEADS
MOBA_BLOCK = 256
MOBA_TOPK = 3
RET_HEADS = 8
RET_DK = D_MODEL // RET_HEADS
RET_DV = 2 * D_MODEL // RET_HEADS
RET_CHUNK = 128
ROT_BASE = 10000.0
N_EXPERTS = 64
TOP_K = 6
D_EXPERT = D_MODEL // 4
D_SHARED = D_MODEL // 4
ROUTED_SCALE = 2.5
LN_EPS = 1e-5
N_MOD = 6

ATTN_Q_W = N_HEADS * HEAD_DIM
ATTN_KV_W = KV_HEADS * HEAD_DIM
RET_QK_W = RET_HEADS * RET_DK
RET_V_W = RET_HEADS * RET_DV
OFF_QA = 0
OFF_KA = OFF_QA + ATTN_Q_W
OFF_VA = OFF_KA + ATTN_KV_W
OFF_QR = OFF_VA + ATTN_KV_W
OFF_KR = OFF_QR + RET_QK_W
OFF_VR = OFF_KR + RET_QK_W
OFF_GR = OFF_VR + RET_V_W
OFF_BA = OFF_GR + RET_V_W
OFF_BR = OFF_BA + D_MODEL
D_IN = OFF_BR + D_MODEL

LANES = 128
SUBLANES = 8
VMEM_LIMIT_BYTES = 52 * 1024 * 1024
MOE_ROWS = 256
ROW_TILES = D_MODEL // LANES

NEG_INF = float("-inf")


def _params(n_axes):
    return pltpu.CompilerParams(dimension_semantics=("arbitrary",) * n_axes,
                                vmem_limit_bytes=VMEM_LIMIT_BYTES)


def _silu(x):
    return x * jax.nn.sigmoid(x)


def _ada_kernel(c_ref, w_ref, b_ref, o_ref):
    a = _silu(c_ref[...]).astype(BF16)
    o_ref[...] = jnp.dot(a, w_ref[...].astype(BF16), preferred_element_type=F32) + b_ref[...]


def _adaln(c_all, w_ada, b_ada):
    depth, d, n = w_ada.shape
    r = c_all.shape[0]
    tn = 1024
    return pl.pallas_call(
        _ada_kernel,
        out_shape=jax.ShapeDtypeStruct((depth, r, n), F32),
        grid=(depth, n // tn),
        in_specs=[pl.BlockSpec((r, d), lambda l, j: (0, 0)),
                  pl.BlockSpec((None, d, tn), lambda l, j: (l, 0, j)),
                  pl.BlockSpec((None, 1, tn), lambda l, j: (l, 0, j))],
        out_specs=pl.BlockSpec((None, r, tn), lambda l, j: (l, 0, j)),
        compiler_params=_params(2),
        name="adaln",
    )(c_all, w_ada, b_ada.reshape(depth, 1, n))


def _row_spec(tr, d, per_row):
    if per_row:
        return pl.BlockSpec((None, tr, d), lambda g, i: (g, i, 0))
    return pl.BlockSpec((None, 1, d), lambda g, i: (g, 0, 0))


def _modulate_kernel(x_ref, sc_ref, sh_ref, o_ref):
    o_ref[...] = (x_ref[...] * (1.0 + sc_ref[...]) + sh_ref[...]).astype(o_ref.dtype)


def _modulate(x, sc, sh, out_dtype):
    g, r, d = x.shape
    tr = min(r, 512)
    per_row = sc.shape[1] == r and r > 1
    return pl.pallas_call(
        _modulate_kernel,
        out_shape=jax.ShapeDtypeStruct((g, r, d), out_dtype),
        grid=(g, r // tr),
        in_specs=[pl.BlockSpec((None, tr, d), lambda g, i: (g, i, 0)),
                  _row_spec(tr, d, per_row), _row_spec(tr, d, per_row)],
        out_specs=pl.BlockSpec((None, tr, d), lambda g, i: (g, i, 0)),
        compiler_params=_params(2),
        name="modulate",
    )(x, sc, sh)


def _ln_kernel(alpha, has_mod, x_ref, y_ref, gt_ref, g_ref, b_ref, *rest):
    if has_mod:
        sc_ref, sh_ref, xo_ref, ho_ref = rest
    else:
        (xo_ref,) = rest
    z = alpha * x_ref[...] + gt_ref[...] * y_ref[...]
    mu = jnp.mean(z, axis=-1, keepdims=True)
    dz = z - mu
    var = jnp.mean(dz * dz, axis=-1, keepdims=True)
    xn = dz * lax.rsqrt(var + LN_EPS) * g_ref[...] + b_ref[...]
    xo_ref[...] = xn
    if has_mod:
        ho_ref[...] = (xn * (1.0 + sc_ref[...]) + sh_ref[...]).astype(ho_ref.dtype)


def _post_ln(alpha, x, y, gate, ln_g, ln_b, sc=None, sh=None, h_dtype=None):
    g, r, d = x.shape
    tr = min(r, 256)
    per_row = gate.shape[1] == r and r > 1
    has_mod = sc is not None
    tile = pl.BlockSpec((None, tr, d), lambda g, i: (g, i, 0))
    vec = pl.BlockSpec((1, d), lambda g, i: (0, 0))
    in_specs = [tile, tile, _row_spec(tr, d, per_row), vec, vec]
    args = [x, y, gate, ln_g.reshape(1, d), ln_b.reshape(1, d)]
    out_shape = [jax.ShapeDtypeStruct((g, r, d), F32)]
    out_specs = [tile]
    if has_mod:
        in_specs += [_row_spec(tr, d, per_row), _row_spec(tr, d, per_row)]
        args += [sc, sh]
        out_shape.append(jax.ShapeDtypeStruct((g, r, d), h_dtype))
        out_specs.append(tile)
    res = pl.pallas_call(
        functools.partial(_ln_kernel, alpha, has_mod),
        out_shape=out_shape,
        grid=(g, r // tr),
        in_specs=in_specs,
        out_specs=out_specs,
        compiler_params=_params(2),
        name="post_ln",
    )(*args)
    return res if has_mod else (res[0], None)


def _mm_kernel(has_gate, has_add, a_ref, w_ref, *rest):
    rest = list(rest)
    gate_ref = rest.pop(0) if has_gate else None
    add_ref = rest.pop(0) if has_add else None
    o_ref, wbf = rest

    @pl.when(pl.program_id(1) == 0)
    def _():
        wbf[...] = w_ref[...].astype(BF16)

    acc = jnp.dot(a_ref[...], wbf[...], preferred_element_type=F32)
    if has_gate:
        acc = acc * jax.nn.sigmoid(gate_ref[...])
    if has_add:
        acc = acc + add_ref[...]
    o_ref[...] = acc.astype(o_ref.dtype)


def _mm(a, w, layer, *, tn, out_dtype, gate=None, gate_col=0, add=None):
    m, k = a.shape
    n = w.shape[2]
    tm = min(m, 512)
    in_specs = [pl.BlockSpec((tm, k), lambda j, i: (i, 0)),
                pl.BlockSpec((None, k, tn), lambda j, i: (layer, 0, j))]
    args = [a, w]
    if gate is not None:
        goff = gate_col // tn
        in_specs.append(pl.BlockSpec((tm, tn), lambda j, i: (i, goff + j)))
        args.append(gate)
    if add is not None:
        in_specs.append(pl.BlockSpec((tm, tn), lambda j, i: (i, j)))
        args.append(add)
    return pl.pallas_call(
        functools.partial(_mm_kernel, gate is not None, add is not None),
        out_shape=jax.ShapeDtypeStruct((m, n), out_dtype),
        grid=(n // tn, m // tm),
        in_specs=in_specs,
        out_specs=pl.BlockSpec((tm, tn), lambda j, i: (i, j)),
        scratch_shapes=[pltpu.VMEM((k, tn), BF16)],
        compiler_params=_params(2),
        name="matmul",
    )(*args)


def _select_topk(score, lane, k, n_lanes):
    sel = jnp.zeros(score.shape, dtype=jnp.bool_)
    picks = []
    for _ in range(k):
        m = jnp.max(score, axis=-1, keepdims=True)
        cand = jnp.where((score == m) & (m > NEG_INF), lane, n_lanes)
        idx = jnp.min(cand, axis=-1, keepdims=True)
        pick = lane == idx
        sel = sel | pick
        score = jnp.where(pick, NEG_INF, score)
        picks.append(pick)
    return sel, picks


def _moba_prompt_kernel(nb, q_ref, k_ref, v_ref, o_ref, kbf, vbf, kmean):
    i = pl.program_id(2)
    scale = HEAD_DIM ** -0.5

    @pl.when(i == 0)
    def _():
        kf = k_ref[...]
        kbf[...] = kf.astype(BF16)
        vbf[...] = v_ref[...].astype(BF16)
        kmean[...] = jnp.mean(kf.reshape(nb, MOBA_BLOCK, HEAD_DIM), axis=1)

    q = q_ref[...]
    qq = jnp.concatenate([q[:, :HEAD_DIM], q[:, HEAD_DIM:]], axis=0)
    rows = qq.shape[0]
    gate = lax.dot_general(qq, kmean[...], (((1,), (1,)), ((), ())),
                           precision=lax.Precision.HIGHEST, preferred_element_type=F32)
    blk = lax.broadcasted_iota(jnp.int32, gate.shape, 1)
    gate = jnp.where(blk < i, gate, NEG_INF)
    sel, _ = _select_topk(gate, blk, MOBA_TOPK, nb)
    self = sel.astype(F32)

    qb = qq.astype(BF16)

    def scores(j):
        kj = kbf[pl.ds(pl.multiple_of(j * MOBA_BLOCK, MOBA_BLOCK), MOBA_BLOCK), :]
        return lax.dot_general(qb, kj, (((1,), (1,)), ((), ())), preferred_element_type=F32) * scale

    def values(j):
        return vbf[pl.ds(pl.multiple_of(j * MOBA_BLOCK, MOBA_BLOCK), MOBA_BLOCK), :]

    s = scores(i)
    qpos = lax.broadcasted_iota(jnp.int32, s.shape, 0) % MOBA_BLOCK
    kpos = lax.broadcasted_iota(jnp.int32, s.shape, 1)
    s = jnp.where(kpos <= qpos, s, NEG_INF)
    m0 = jnp.max(s, axis=-1, keepdims=True)
    p = jnp.exp(s - m0)
    l0 = jnp.sum(p, axis=-1, keepdims=True)
    acc0 = jnp.dot(p.astype(BF16), values(i), preferred_element_type=F32)

    def body(j, carry):
        m, l, acc = carry
        sj = jnp.max(jnp.where(blk == j, self, 0.0), axis=-1, keepdims=True)
        s = jnp.where(sj > 0.0, scores(j), NEG_INF)
        m_new = jnp.maximum(m, jnp.max(s, axis=-1, keepdims=True))
        a = jnp.exp(m - m_new)
        p = jnp.exp(s - m_new)
        l = a * l + jnp.sum(p, axis=-1, keepdims=True)
        acc = a * acc + jnp.dot(p.astype(BF16), values(j), preferred_element_type=F32)
        return m_new, l, acc

    _, l, acc = lax.fori_loop(0, i, body, (m0, l0, acc0))
    o = acc / l
    half = rows // 2
    o_ref[...] = jnp.concatenate([o[:half], o[half:]], axis=1).astype(o_ref.dtype)


def _moba_prompt(p_all, bsz, seq):
    nb = seq // MOBA_BLOCK
    qw = Q_PER_KV * HEAD_DIM
    return pl.pallas_call(
        functools.partial(_moba_prompt_kernel, nb),
        out_shape=jax.ShapeDtypeStruct((bsz * seq, ATTN_Q_W), BF16),
        grid=(bsz, KV_HEADS, nb),
        in_specs=[pl.BlockSpec((MOBA_BLOCK, qw), lambda b, g, i: (b * nb + i, OFF_QA // qw + g)),
                  pl.BlockSpec((seq, HEAD_DIM), lambda b, g, i: (b, OFF_KA // HEAD_DIM + g)),
                  pl.BlockSpec((seq, HEAD_DIM), lambda b, g, i: (b, OFF_VA // HEAD_DIM + g))],
        out_specs=pl.BlockSpec((MOBA_BLOCK, qw), lambda b, g, i: (b * nb + i, g)),
        scratch_shapes=[pltpu.VMEM((seq, HEAD_DIM), BF16), pltpu.VMEM((seq, HEAD_DIM), BF16),
                        pltpu.VMEM((nb, HEAD_DIM), F32)],
        compiler_params=_params(3),
        name="moba_prompt",
    )(p_all, p_all, p_all)


def _moba_sample_kernel(n_blocks, pages_per_step, pt_ref, q_ref, kn_ref, vn_ref, *rest):
    k_pages = rest[:pages_per_step]
    v_pages = rest[pages_per_step:2 * pages_per_step]
    o_ref, gate_s, m_s, l_s, o_s = rest[2 * pages_per_step:]
    c = pl.program_id(1)
    scale = HEAD_DIM ** -0.5
    page_rows = k_pages[0].shape[0]
    pages_per_block = MOBA_BLOCK // page_rows
    blocks_per_step = pages_per_step // pages_per_block

    q = q_ref[...]
    qb = q.astype(BF16)
    head = lax.broadcasted_iota(jnp.int32, q.shape, 0)
    lane = lax.broadcasted_iota(jnp.int32, q.shape, 1)

    @pl.when(c == 0)
    def _():
        gate_s[...] = jnp.zeros_like(gate_s)
        m_s[...] = jnp.zeros_like(m_s)
        l_s[...] = jnp.zeros_like(l_s)

    for b2 in range(blocks_per_step):
        blk = c * blocks_per_step + b2
        gt = jnp.zeros((N_HEADS, 1), F32)
        mb = jnp.zeros((N_HEADS, 1), F32)
        lb = jnp.zeros((N_HEADS, 1), F32)
        ob = jnp.zeros((N_HEADS, HEAD_DIM), F32)
        for g in range(KV_HEADS):
            cols = slice(g * HEAD_DIM, (g + 1) * HEAD_DIM)
            pages = range(b2 * pages_per_block, (b2 + 1) * pages_per_block)
            kg = jnp.concatenate([k_pages[t][:, cols] for t in pages], axis=0)
            vg = jnp.concatenate([v_pages[t][:, cols] for t in pages], axis=0)
            kmean = jnp.mean(kg, axis=0, keepdims=True)
            gt_g = jnp.sum(q * kmean, axis=-1, keepdims=True)
            s = lax.dot_general(qb, kg.astype(BF16), (((1,), (1,)), ((), ())),
                                preferred_element_type=F32) * scale
            mb_g = jnp.max(s, axis=-1, keepdims=True)
            p = jnp.exp(s - mb_g)
            lb_g = jnp.sum(p, axis=-1, keepdims=True)
            ob_g = jnp.dot(p.astype(BF16), vg.astype(BF16), preferred_element_type=F32)
            mine = (head[:, :1] // Q_PER_KV) == g
            gt = jnp.where(mine, gt_g, gt)
            mb = jnp.where(mine, mb_g, mb)
            lb = jnp.where(mine, lb_g, lb)
            ob = jnp.where(mine, ob_g, ob)
        here = lane == blk
        gate_s[...] = jnp.where(here, gt, gate_s[...])
        m_s[...] = jnp.where(here, mb, m_s[...])
        l_s[...] = jnp.where(here, lb, l_s[...])
        o_s[blk] = ob

    @pl.when(c == pl.num_programs(1) - 1)
    def _():
        gate = jnp.where(lane < n_blocks, gate_s[...], NEG_INF)
        sel, _ = _select_topk(gate, lane, MOBA_TOPK, LANES)
        s_self = jnp.sum(q * kn_ref[...], axis=-1, keepdims=True) * scale
        m_all = m_s[...]
        m_fin = jnp.maximum(jnp.max(jnp.where(sel, m_all, NEG_INF), axis=-1, keepdims=True), s_self)
        w = jnp.where(sel, jnp.exp(m_all - m_fin), 0.0)
        w_self = jnp.exp(s_self - m_fin)
        denom = jnp.sum(w * l_s[...], axis=-1, keepdims=True) + w_self
        num = w_self * vn_ref[...]
        for b in range(n_blocks):
            wb = jnp.sum(jnp.where(lane == b, w, 0.0), axis=-1, keepdims=True)
            num = num + wb * o_s[b]
        o_ref[...] = num / denom


def _moba_sample(q, k_new, v_new, cache_k, cache_v, page_table, layer):
    s_cnt = q.shape[0]
    page_rows = cache_k.shape[2]
    n_pages = page_table.shape[0] // s_cnt
    n_blocks = n_pages * page_rows // MOBA_BLOCK
    pages_per_step = 4
    steps = n_pages // pages_per_step

    def page_spec(t):
        return pl.BlockSpec((None, None, page_rows, ATTN_KV_W),
                            lambda s, c, pt: (layer, pt[s * n_pages + c * pages_per_step + t], 0, 0))

    head_spec = pl.BlockSpec((None, N_HEADS, HEAD_DIM), lambda s, c, pt: (s, 0, 0))
    grid_spec = pltpu.PrefetchScalarGridSpec(
        num_scalar_prefetch=1,
        grid=(s_cnt, steps),
        in_specs=[head_spec, head_spec, head_spec]
                 + [page_spec(t) for t in range(pages_per_step)] * 2,
        out_specs=head_spec,
        scratch_shapes=[pltpu.VMEM((N_HEADS, LANES), F32), pltpu.VMEM((N_HEADS, LANES), F32),
                        pltpu.VMEM((N_HEADS, LANES), F32), pltpu.VMEM((n_blocks, N_HEADS, HEAD_DIM), F32)],
    )
    return pl.pallas_call(
        functools.partial(_moba_sample_kernel, n_blocks, pages_per_step),
        out_shape=jax.ShapeDtypeStruct((s_cnt, N_HEADS, HEAD_DIM), F32),
        grid_spec=grid_spec,
        compiler_params=_params(2),
        name="moba_sample",
    )(page_table, q, k_new, v_new, *([cache_k] * pages_per_step), *([cache_v] * pages_per_step))


def _group_norm_gate(o, gn, gr):
    mu = jnp.mean(o, axis=-1, keepdims=True)
    do = o - mu
    var = jnp.mean(do * do, axis=-1, keepdims=True)
    return do * lax.rsqrt(var + LN_EPS) * gn * _silu(gr)


def _ret_prompt_kernel(q_ref, k_ref, v_ref, g_ref, cos_ref, sin_ref, dmask_ref, qdec_ref, kdec_ref, sdec_ref,
                       gn_ref, o_ref, s_ref):
    @pl.when(pl.program_id(2) == 0)
    def _():
        s_ref[...] = jnp.zeros_like(s_ref)

    cos = cos_ref[...]
    sin = sin_ref[...]
    half = RET_DK // 2

    def rot(x):
        x1, x2 = x[:, :half], x[:, half:]
        return jnp.concatenate([x1 * cos - x2 * sin, x2 * cos + x1 * sin], axis=1)

    q = rot(q_ref[...])
    k = rot(k_ref[...]) * (RET_DK ** -0.5)
    qb = q.astype(BF16)
    vb = v_ref[...].astype(BF16)
    state = s_ref[...]
    scores = lax.dot_general(qb, k.astype(BF16), (((1,), (1,)), ((), ())),
                             preferred_element_type=F32) * dmask_ref[...]
    o = jnp.dot(scores.astype(BF16), vb, preferred_element_type=F32)
    o = o + jnp.dot(qb, state.astype(BF16), preferred_element_type=F32) * qdec_ref[...]
    kd = (k * kdec_ref[...]).astype(BF16)
    s_ref[...] = state * sdec_ref[...] + lax.dot_general(kd, vb, (((0,), (0,)), ((), ())),
                                                         preferred_element_type=F32)
    o_ref[...] = _group_norm_gate(o, gn_ref[...], g_ref[...]).astype(o_ref.dtype)


def _decay_tables(c):
    lg = jnp.log1p(-jnp.exp2(-5.0 - jnp.arange(RET_HEADS, dtype=F32)))
    i = jnp.arange(c, dtype=F32)
    diff = i[:, None] - i[None, :]
    causal = diff >= 0.0
    dmask = jnp.where(causal[None], jnp.exp(jnp.where(causal, diff, 0.0)[None] * lg[:, None, None]), 0.0)
    qdec = jnp.exp((i[:, None] + 1.0) * lg[None, :]).T[:, :, None]
    kdec = jnp.exp((c - 1.0 - i)[:, None] * lg[None, :]).T[:, :, None]
    sdec = jnp.exp(c * lg)
    return dmask, qdec, kdec, sdec


def _rot_tables(pos):
    half = RET_DK // 2
    theta = ROT_BASE ** -jnp.linspace(0.0, 1.0, half, dtype=F32)
    ang = pos.astype(F32)[:, None] * theta[None, :]
    return jnp.cos(ang), jnp.sin(ang)


def _ret_prompt(p_all, bsz, seq, gn):
    c = min(RET_CHUNK, seq)
    nc = seq // c
    cos, sin = _rot_tables(jnp.arange(seq, dtype=jnp.int32))
    dmask, qdec, kdec, sdec = _decay_tables(c)
    head_vec = lambda b, h, j: (h, 0, 0)
    return pl.pallas_call(
        _ret_prompt_kernel,
        out_shape=[jax.ShapeDtypeStruct((bsz * seq, RET_V_W), BF16),
                   jax.ShapeDtypeStruct((bsz, RET_HEADS, RET_DK, RET_DV), F32)],
        grid=(bsz, RET_HEADS, nc),
        in_specs=[pl.BlockSpec((c, RET_DK), lambda b, h, j: (b * nc + j, OFF_QR // RET_DK + h)),
                  pl.BlockSpec((c, RET_DK), lambda b, h, j: (b * nc + j, OFF_KR // RET_DK + h)),
                  pl.BlockSpec((c, RET_DV), lambda b, h, j: (b * nc + j, OFF_VR // RET_DV + h)),
                  pl.BlockSpec((c, RET_DV), lambda b, h, j: (b * nc + j, OFF_GR // RET_DV + h)),
                  pl.BlockSpec((c, RET_DK // 2), lambda b, h, j: (j, 0)),
                  pl.BlockSpec((c, RET_DK // 2), lambda b, h, j: (j, 0)),
                  pl.BlockSpec((None, c, c), head_vec),
                  pl.BlockSpec((None, c, 1), head_vec),
                  pl.BlockSpec((None, c, 1), head_vec),
                  pl.BlockSpec((None, 1, 1), head_vec),
                  pl.BlockSpec((1, RET_DV), lambda b, h, j: (0, h))],
        out_specs=[pl.BlockSpec((c, RET_DV), lambda b, h, j: (b * nc + j, h)),
                   pl.BlockSpec((None, None, RET_DK, RET_DV), lambda b, h, j: (b, h, 0, 0))],
        compiler_params=_params(3),
        name="retention_prompt",
    )(p_all, p_all, p_all, p_all, cos, sin, dmask, qdec, kdec, sdec.reshape(RET_HEADS, 1, 1),
      gn.reshape(1, RET_V_W))


def _ret_sample_kernel(gam_ref, q_ref, k_ref, v_ref, g_ref, cos_ref, sin_ref, gn_ref, s_in, *rest):
    o_ref, s_out = rest[-2:]
    cos = cos_ref[...]
    sin = sin_ref[...]
    half = RET_DK // 2

    def rot(x):
        x1, x2 = x[:half], x[half:]
        return jnp.concatenate([x1 * cos - x2 * sin, x2 * cos + x1 * sin], axis=0)

    qc = rot(q_ref[...])
    kc = rot(k_ref[...]) * (RET_DK ** -0.5)
    v = v_ref[...]
    gr = g_ref[...]
    gn = gn_ref[...]
    for h in range(RET_HEADS):
        cols = slice(h * RET_DV, (h + 1) * RET_DV)
        new_state = s_in[h] * gam_ref[h] + kc[:, h:h + 1] * v[:, cols]
        s_out[h] = new_state
        o = jnp.sum(qc[:, h:h + 1] * new_state, axis=0, keepdims=True)
        o_ref[:, cols] = _group_norm_gate(o, gn[:, cols], gr[:, cols])


def _ret_sample(q_t, k_t, v, g, state_ret, layer, pos, gn, prev_out):
    s_cnt = q_t.shape[0]
    cos, sin = _rot_tables(jnp.full((1,), pos, dtype=jnp.int32))
    cos, sin = cos.reshape(-1, 1), sin.reshape(-1, 1)
    gam = _decay_tables(1)[3]
    col_spec = pl.BlockSpec((None, RET_DK, RET_HEADS), lambda s: (s, 0, 0))
    row_spec = pl.BlockSpec((None, 1, RET_V_W), lambda s: (s, 0, 0))
    rot_spec = pl.BlockSpec((RET_DK // 2, 1), lambda s: (0, 0))
    state_spec = pl.BlockSpec((None, None, RET_HEADS, RET_DK, RET_DV), lambda s: (layer, s, 0, 0, 0))
    in_specs = [pl.BlockSpec(memory_space=pltpu.SMEM), col_spec, col_spec, row_spec, row_spec, rot_spec, rot_spec,
                pl.BlockSpec((1, RET_V_W), lambda s: (0, 0)), state_spec]
    args = [gam, q_t, k_t, v, g, cos, sin, gn.reshape(1, RET_V_W), state_ret]
    aliases = {}
    if prev_out is not None:
        in_specs.append(pl.BlockSpec(memory_space=pl.ANY))
        args.append(prev_out)
        aliases = {len(args) - 1: 1}
    return pl.pallas_call(
        _ret_sample_kernel,
        out_shape=[jax.ShapeDtypeStruct((s_cnt, 1, RET_V_W), F32),
                   jax.ShapeDtypeStruct(state_ret.shape, state_ret.dtype)],
        grid=(s_cnt,),
        in_specs=in_specs,
        out_specs=[row_spec, state_spec],
        input_output_aliases=aliases,
        compiler_params=_params(1),
        name="retention_sample",
    )(*args)


def _router_kernel(h_ref, w_ref, b_ref, idx_ref, wt_ref):
    logits = jnp.dot(h_ref[...], w_ref[...], precision=lax.Precision.HIGHEST, preferred_element_type=F32)
    aff = jax.nn.sigmoid(logits)
    lane = lax.broadcasted_iota(jnp.int32, aff.shape, 1)
    _, picks = _select_topk(aff + b_ref[...], lane, TOP_K, N_EXPERTS)
    out_lane = lax.broadcasted_iota(jnp.int32, idx_ref.shape, 1)
    idx = jnp.zeros(idx_ref.shape, jnp.int32)
    top_aff = jnp.zeros(idx_ref.shape, F32)
    for t, pick in enumerate(picks):
        e_t = jnp.sum(jnp.where(pick, lane, 0), axis=-1, keepdims=True)
        a_t = jnp.sum(jnp.where(pick, aff, 0.0), axis=-1, keepdims=True)
        idx = jnp.where(out_lane == t, e_t, idx)
        top_aff = jnp.where(out_lane == t, a_t, top_aff)
    idx_ref[...] = idx
    wt_ref[...] = top_aff / jnp.sum(top_aff, axis=-1, keepdims=True) * ROUTED_SCALE


def _router(h, w_router_l, bias_l):
    m, d = h.shape
    tm = min(m, 512)
    out = pl.BlockSpec((tm, LANES), lambda i: (i, 0))
    return pl.pallas_call(
        _router_kernel,
        out_shape=[jax.ShapeDtypeStruct((m, LANES), jnp.int32), jax.ShapeDtypeStruct((m, LANES), F32)],
        grid=(m // tm,),
        in_specs=[pl.BlockSpec((tm, d), lambda i: (i, 0)),
                  pl.BlockSpec((d, N_EXPERTS), lambda i: (0, 0)),
                  pl.BlockSpec((1, N_EXPERTS), lambda i: (0, 0))],
        out_specs=[out, out],
        compiler_params=_params(1),
        name="router",
    )(h, w_router_l, bias_l.reshape(1, N_EXPERTS))


def _swiglu_bf16(x, wg, wu, wd):
    g = jnp.dot(x, wg, preferred_element_type=F32)
    u = jnp.dot(x, wu, preferred_element_type=F32)
    return jnp.dot((_silu(g) * u).astype(BF16), wd, preferred_element_type=F32)


def _shared_kernel(h_ref, wg_ref, wu_ref, wd_ref, o_ref, wgb, wub, wdb):
    @pl.when(pl.program_id(0) == 0)
    def _():
        wgb[...] = wg_ref[...].astype(BF16)
        wub[...] = wu_ref[...].astype(BF16)
        wdb[...] = wd_ref[...].astype(BF16)

    o_ref[...] = _swiglu_bf16(h_ref[...].astype(BF16), wgb[...], wub[...], wdb[...])


def _shared_expert(h, wg, wu, wd, layer):
    m, d = h.shape
    tm = min(m, 512)
    return pl.pallas_call(
        _shared_kernel,
        out_shape=jax.ShapeDtypeStruct((m, d), F32),
        grid=(m // tm,),
        in_specs=[pl.BlockSpec((tm, d), lambda i: (i, 0)),
                  pl.BlockSpec((None, d, D_SHARED), lambda i: (layer, 0, 0)),
                  pl.BlockSpec((None, d, D_SHARED), lambda i: (layer, 0, 0)),
                  pl.BlockSpec((None, D_SHARED, d), lambda i: (layer, 0, 0))],
        out_specs=pl.BlockSpec((tm, d), lambda i: (i, 0)),
        scratch_shapes=[pltpu.VMEM((d, D_SHARED), BF16), pltpu.VMEM((d, D_SHARED), BF16),
                        pltpu.VMEM((D_SHARED, d), BF16)],
        compiler_params=_params(1),
        name="shared_expert",
    )(h, wg, wu, wd)


def _experts_kernel(blk_e, src, dst, n_used, h_hbm, wg_ref, wu_ref, wd_ref, y_hbm,
                    xs, ys, xb, wgb, wub, wdb, gsem, ssem):
    i = pl.program_id(0)

    @pl.when(i < n_used[0])
    def _():
        base = i * MOE_ROWS

        def gather(r):
            return pltpu.make_async_copy(h_hbm.at[src[base + r]], xs.at[r], gsem)

        def scatter(r):
            return pltpu.make_async_copy(ys.at[r], y_hbm.at[dst[base + r]], ssem)

        def start_gather(r, carry):
            gather(r).start()
            return carry

        lax.fori_loop(0, MOE_ROWS, start_gather, 0)

        prev = jnp.maximum(i - 1, 0)

        @pl.when((i == 0) | (blk_e[i] != blk_e[prev]))
        def _():
            wgb[...] = wg_ref[...].astype(BF16)
            wub[...] = wu_ref[...].astype(BF16)
            wdb[...] = wd_ref[...].astype(BF16)

        def wait_gather(r, carry):
            gather(r).wait()
            return carry

        lax.fori_loop(0, MOE_ROWS, wait_gather, 0)

        for t in range(ROW_TILES):
            xb[:, t * LANES:(t + 1) * LANES] = xs[:, t, :].astype(BF16)
        y = _swiglu_bf16(xb[...], wgb[...], wub[...], wdb[...])
        for t in range(ROW_TILES):
            ys[:, t, :] = y[:, t * LANES:(t + 1) * LANES]

        def start_scatter(r, carry):
            scatter(r).start()
            return carry

        lax.fori_loop(0, MOE_ROWS, start_scatter, 0)

        def wait_scatter(r, carry):
            scatter(r).wait()
            return carry

        lax.fori_loop(0, MOE_ROWS, wait_scatter, 0)


def _routed_experts(h_rows, top_idx, wg, wu, wd, layer, n_out_rows):
    m = h_rows.shape[0]
    a = m * TOP_K
    n_blk = -(-(a + N_EXPERTS * (MOE_ROWS - 1)) // MOE_ROWS)
    n_rows = n_blk * MOE_ROWS
    flat_e = top_idx.reshape(a)
    order = jnp.argsort(flat_e).astype(jnp.int32)
    e_sorted = flat_e[order]
    counts = jnp.zeros((N_EXPERTS,), jnp.int32).at[flat_e].add(1)
    padded = (counts + MOE_ROWS - 1) // MOE_ROWS * MOE_ROWS
    start = jnp.cumsum(counts) - counts
    ends = jnp.cumsum(padded)
    pstart = ends - padded
    dest = pstart[e_sorted] + jnp.arange(a, dtype=jnp.int32) - start[e_sorted]
    src = jnp.zeros((n_rows,), jnp.int32).at[dest].set(order // TOP_K)
    dst = (a + jnp.arange(n_rows, dtype=jnp.int32) % MOE_ROWS).at[dest].set(order)
    blk_e = jnp.minimum(jnp.searchsorted(ends, jnp.arange(n_blk, dtype=jnp.int32) * MOE_ROWS, side='right'),
                        N_EXPERTS - 1).astype(jnp.int32)
    n_used = (ends[-1:] // MOE_ROWS).astype(jnp.int32)

    d = D_MODEL
    grid_spec = pltpu.PrefetchScalarGridSpec(
        num_scalar_prefetch=4,
        grid=(n_blk,),
        in_specs=[pl.BlockSpec(memory_space=pl.ANY),
                  pl.BlockSpec((None, None, d, D_EXPERT), lambda i, be, s, t, n: (layer, be[i], 0, 0)),
                  pl.BlockSpec((None, None, d, D_EXPERT), lambda i, be, s, t, n: (layer, be[i], 0, 0)),
                  pl.BlockSpec((None, None, D_EXPERT, d), lambda i, be, s, t, n: (layer, be[i], 0, 0))],
        out_specs=pl.BlockSpec(memory_space=pl.ANY),
        scratch_shapes=[pltpu.VMEM((MOE_ROWS, ROW_TILES, LANES), F32),
                        pltpu.VMEM((MOE_ROWS, ROW_TILES, LANES), F32),
                        pltpu.VMEM((MOE_ROWS, d), BF16),
                        pltpu.VMEM((d, D_EXPERT), BF16), pltpu.VMEM((d, D_EXPERT), BF16),
                        pltpu.VMEM((D_EXPERT, d), BF16),
                        pltpu.SemaphoreType.DMA(()), pltpu.SemaphoreType.DMA(())],
    )
    return pl.pallas_call(
        _experts_kernel,
        out_shape=jax.ShapeDtypeStruct((n_out_rows, ROW_TILES, LANES), F32),
        grid_spec=grid_spec,
        compiler_params=_params(1),
        name="routed_experts",
    )(blk_e, src, dst, n_used, h_rows, wg, wu, wd)


def _combine_kernel(alpha, has_mod, y_ref, wt_ref, sh_ref, x_ref, gt_ref, g_ref, b_ref, *rest):
    if has_mod:
        sc_ref, shift_ref, xo_ref, ho_ref, z_s = rest
    else:
        xo_ref, z_s = rest
    wt = wt_ref[...]
    for t in range(ROW_TILES):
        cols = slice(t * LANES, (t + 1) * LANES)
        routed = wt[:, 0:1] * y_ref[:, 0, t, :]
        for k in range(1, TOP_K):
            routed = routed + wt[:, k:k + 1] * y_ref[:, k, t, :]
        z_s[:, cols] = sh_ref[:, cols] + routed
    z = alpha * x_ref[...] + gt_ref[...] * z_s[...]
    mu = jnp.mean(z, axis=-1, keepdims=True)
    dz = z - mu
    var = jnp.mean(dz * dz, axis=-1, keepdims=True)
    xn = dz * lax.rsqrt(var + LN_EPS) * g_ref[...] + b_ref[...]
    xo_ref[...] = xn
    if has_mod:
        ho_ref[...] = (xn * (1.0 + sc_ref[...]) + shift_ref[...]).astype(ho_ref.dtype)


def _combine_ln(alpha, y_rows, top_w, shared, x, gate, ln_g, ln_b, tok0, sc=None, sh=None):
    g, r, d = x.shape
    tr = min(r, 128)
    per_row = gate.shape[1] == r and r > 1
    has_mod = sc is not None
    nt = r // tr
    t0 = tok0 // tr
    tile = pl.BlockSpec((None, tr, d), lambda g, i: (g, i, 0))
    flat = pl.BlockSpec((tr, d), lambda g, i: (g * nt + i, 0))
    vec = pl.BlockSpec((1, d), lambda g, i: (0, 0))
    in_specs = [pl.BlockSpec((tr, TOP_K, ROW_TILES, LANES), lambda g, i: (t0 + g * nt + i, 0, 0, 0)),
                pl.BlockSpec((tr, LANES), lambda g, i: (g * nt + i, 0)),
                flat, tile, _row_spec(tr, d, per_row), vec, vec]
    args = [y_rows, top_w, shared, x, gate, ln_g.reshape(1, d), ln_b.reshape(1, d)]
    out_shape = [jax.ShapeDtypeStruct((g, r, d), F32)]
    out_specs = [tile]
    if has_mod:
        in_specs += [_row_spec(tr, d, per_row), _row_spec(tr, d, per_row)]
        args += [sc, sh]
        out_shape.append(jax.ShapeDtypeStruct((g, r, d), BF16))
        out_specs.append(tile)
    res = pl.pallas_call(
        functools.partial(_combine_kernel, alpha, has_mod),
        out_shape=out_shape,
        grid=(g, nt),
        in_specs=in_specs,
        out_specs=out_specs,
        scratch_shapes=[pltpu.VMEM((tr, d), F32)],
        compiler_params=_params(2),
        name="moe_combine_ln",
    )(*args)
    return res if has_mod else (res[0], None)


def kernel(x_prompt, x_sample, cache_k, cache_v, state_ret, page_table, c_prompt, c_sample, w_ada, b_ada, w_in, w_attn_br, ret_norm_g, w_ret_br, w_o, ln_g, ln_b, w_router, router_bias, w_exp_gate, w_exp_up, w_exp_down, w_sh_gate, w_sh_up, w_sh_down):
    bsz, seq, d = x_prompt.shape
    s_cnt = x_sample.shape[0]
    depth = w_in.shape[0]
    alpha = (2.0 * depth) ** 0.25
    n_phys, page_rows = cache_k.shape[1], cache_k.shape[2]
    past = page_table.shape[1] * page_rows
    m_p = bsz * seq
    m_all = m_p + s_cnt
    y_tokens = m_all + -(-MOE_ROWS // TOP_K)

    n_c = bsz + s_cnt
    r_c = -(-n_c // SUBLANES) * SUBLANES
    c_all = jnp.concatenate([c_prompt, c_sample, jnp.zeros((r_c - n_c, d), F32)], axis=0)
    mod = _adaln(c_all, w_ada, b_ada)
    mod_p = mod[:, :bsz].reshape(depth, bsz, N_MOD, 1, d)
    mod_s = mod[:, bsz:n_c].reshape(depth, 1, s_cnt, N_MOD, d)

    def mods(layer):
        mp = [mod_p[layer, :, j] for j in range(N_MOD)]
        ms = [mod_s[layer, :, :, j] for j in range(N_MOD)]
        return mp, ms

    cache_k2 = cache_k.reshape(depth, n_phys, page_rows, ATTN_KV_W)
    cache_v2 = cache_v.reshape(depth, n_phys, page_rows, ATTN_KV_W)
    pt_flat = page_table.reshape(-1)

    xp = x_prompt
    xs = x_sample.reshape(1, s_cnt, d)
    mp, ms = mods(0)
    hp = _modulate(xp, mp[1], mp[0], BF16)
    hs = _modulate(xs, ms[1], ms[0], BF16)

    kp, vp, sp, kd, vd = [], [], [], [], []
    state_out = None
    for layer in range(depth):
        mp, ms = mods(layer)
        pp = _mm(hp.reshape(m_p, d), w_in, layer, tn=1024, out_dtype=F32)
        o_a = _moba_prompt(pp, bsz, seq)
        o_r, s_new = _ret_prompt(pp, bsz, seq, ret_norm_g[layer])
        t1 = _mm(o_a, w_attn_br, layer, tn=512, out_dtype=F32, gate=pp, gate_col=OFF_BA)
        mixed = _mm(o_r, w_ret_br, layer, tn=512, out_dtype=BF16, gate=pp, gate_col=OFF_BR, add=t1)
        y_p = _mm(mixed, w_o, layer, tn=512, out_dtype=F32).reshape(bsz, seq, d)
        xp, h2p = _post_ln(alpha, xp, y_p, mp[2], ln_g[layer, 0], ln_b[layer, 0], mp[4], mp[3], F32)
        kp.append(pp[:, OFF_KA:OFF_VA].reshape(bsz, seq, KV_HEADS, HEAD_DIM))
        vp.append(pp[:, OFF_VA:OFF_QR].reshape(bsz, seq, KV_HEADS, HEAD_DIM))
        sp.append(s_new)

        ps = _mm(hs.reshape(s_cnt, d), w_in, layer, tn=1024, out_dtype=F32)
        k_new = ps[:, OFF_KA:OFF_VA].reshape(s_cnt, KV_HEADS, HEAD_DIM)
        v_new = ps[:, OFF_VA:OFF_QR].reshape(s_cnt, KV_HEADS, HEAD_DIM)
        o_as = _moba_sample(ps[:, OFF_QA:OFF_KA].reshape(s_cnt, N_HEADS, HEAD_DIM),
                            jnp.repeat(k_new, Q_PER_KV, axis=1), jnp.repeat(v_new, Q_PER_KV, axis=1),
                            cache_k2, cache_v2, pt_flat, layer)
        q_t = ps[:, OFF_QR:OFF_KR].reshape(s_cnt, RET_HEADS, RET_DK).transpose(0, 2, 1)
        k_t = ps[:, OFF_KR:OFF_VR].reshape(s_cnt, RET_HEADS, RET_DK).transpose(0, 2, 1)
        o_rs, state_out = _ret_sample(q_t, k_t, ps[:, OFF_VR:OFF_GR].reshape(s_cnt, 1, RET_V_W),
                                      ps[:, OFF_GR:OFF_BA].reshape(s_cnt, 1, RET_V_W),
                                      state_ret, layer, past, ret_norm_g[layer], state_out)
        t1s = _mm(o_as.reshape(s_cnt, ATTN_Q_W).astype(BF16), w_attn_br, layer, tn=512, out_dtype=F32,
                  gate=ps, gate_col=OFF_BA)
        mixed_s = _mm(o_rs.reshape(s_cnt, RET_V_W).astype(BF16), w_ret_br, layer, tn=512, out_dtype=BF16,
                      gate=ps, gate_col=OFF_BR, add=t1s)
        y_s = _mm(mixed_s, w_o, layer, tn=512, out_dtype=F32).reshape(1, s_cnt, d)
        xs, h2s = _post_ln(alpha, xs, y_s, ms[2], ln_g[layer, 0], ln_b[layer, 0], ms[4], ms[3], F32)
        kd.append(k_new.reshape(s_cnt, 1, KV_HEADS, HEAD_DIM))
        vd.append(v_new.reshape(s_cnt, 1, KV_HEADS, HEAD_DIM))

        h2p = h2p.reshape(m_p, d)
        h2s = h2s.reshape(s_cnt, d)
        idx_p, wt_p = _router(h2p, w_router[layer], router_bias[layer])
        idx_s, wt_s = _router(h2s, w_router[layer], router_bias[layer])
        sh_p = _shared_expert(h2p, w_sh_gate, w_sh_up, w_sh_down, layer)
        sh_s = _shared_expert(h2s, w_sh_gate, w_sh_up, w_sh_down, layer)
        h_rows = jnp.concatenate([h2p, h2s], axis=0).reshape(m_all, ROW_TILES, LANES)
        top_idx = jnp.concatenate([idx_p[:, :TOP_K], idx_s[:, :TOP_K]], axis=0)
        y_rows = _routed_experts(h_rows, top_idx, w_exp_gate, w_exp_up, w_exp_down, layer, y_tokens * TOP_K)
        y_rows = y_rows.reshape(y_tokens, TOP_K, ROW_TILES, LANES)
        if layer + 1 < depth:
            mp_n, ms_n = mods(layer + 1)
            nxt_p, nxt_s = (mp_n[1], mp_n[0]), (ms_n[1], ms_n[0])
        else:
            nxt_p = nxt_s = (None, None)
        xp, hp = _combine_ln(alpha, y_rows, wt_p, sh_p, xp, mp[5], ln_g[layer, 1], ln_b[layer, 1], 0, *nxt_p)
        xs, hs = _combine_ln(alpha, y_rows, wt_s, sh_s, xs, ms[5], ln_g[layer, 1], ln_b[layer, 1], m_p, *nxt_s)

    return (xp, xs.reshape(s_cnt, 1, d), jnp.stack(kp), jnp.stack(vp), jnp.stack(sp).astype(state_ret.dtype),
            jnp.stack(kd), jnp.stack(vd), state_out)
```

```python
import functools

import jax
import jax.numpy as jnp
import numpy as np
from jax import lax
from jax.experimental import pallas as pl
from jax.experimental.pallas import tpu as pltpu

F32 = jnp.float32
BF16 = jnp.bfloat16

D_MODEL = 2048
N_HEADS = 16
KV_HEADS = 8
HEAD_DIM = 128
Q_PER_KV = N_HEADS // KV_HEADS
MOBA_BLOCK = 256
MOBA_TOPK = 3
RET_HEADS = 8
RET_DK = D_MODEL // RET_HEADS
RET_DV = 2 * D_MODEL // RET_HEADS
RET_CHUNK = 128
ROT_BASE = 10000.0
N_EXPERTS = 64
TOP_K = 6
D_EXPERT = D_MODEL // 4
D_SHARED = D_MODEL // 4
ROUTED_SCALE = 2.5
LN_EPS = 1e-5
N_MOD = 6

ATTN_Q_W = N_HEADS * HEAD_DIM
ATTN_KV_W = KV_HEADS * HEAD_DIM
RET_QK_W = RET_HEADS * RET_DK
RET_V_W = RET_HEADS * RET_DV
OFF_QA = 0
OFF_KA = OFF_QA + ATTN_Q_W
OFF_VA = OFF_KA + ATTN_KV_W
OFF_QR = OFF_VA + ATTN_KV_W
OFF_KR = OFF_QR + RET_QK_W
OFF_VR = OFF_KR + RET_QK_W
OFF_GR = OFF_VR + RET_V_W
OFF_BA = OFF_GR + RET_V_W
OFF_BR = OFF_BA + D_MODEL
D_IN = OFF_BR + D_MODEL

LANES = 128
SUBLANES = 8
VMEM_LIMIT_BYTES = 52 * 1024 * 1024
MOE_ROWS = 256
ROW_TILES = D_MODEL // LANES

NEG_INF = float("-inf")


def _params(n_axes):
    return pltpu.CompilerParams(dimension_semantics=("arbitrary",) * n_axes,
                                vmem_limit_bytes=VMEM_LIMIT_BYTES)


def _silu(x):
    return x * jax.nn.sigmoid(x)


def _ada_kernel(c_ref, w_ref, b_ref, o_ref):
    a = _silu(c_ref[...]).astype(BF16)
    o_ref[...] = jnp.dot(a, w_ref[...].astype(BF16), preferred_element_type=F32) + b_ref[...]


def _adaln(c_all, w_ada, b_ada):
    depth, d, n = w_ada.shape
    r = c_all.shape[0]
    tn = 1024
    return pl.pallas_call(
        _ada_kernel,
        out_shape=jax.ShapeDtypeStruct((depth, r, n), F32),
        grid=(depth, n // tn),
        in_specs=[pl.BlockSpec((r, d), lambda l, j: (0, 0)),
                  pl.BlockSpec((None, d, tn), lambda l, j: (l, 0, j)),
                  pl.BlockSpec((None, 1, tn), lambda l, j: (l, 0, j))],
        out_specs=pl.BlockSpec((None, r, tn), lambda l, j: (l, 0, j)),
        compiler_params=_params(2),
        name="adaln",
    )(c_all, w_ada, b_ada.reshape(depth, 1, n))


def _row_spec(tr, d, per_row):
    if per_row:
        return pl.BlockSpec((None, tr, d), lambda g, i: (g, i, 0))
    return pl.BlockSpec((None, 1, d), lambda g, i: (g, 0, 0))


def _modulate_kernel(x_ref, sc_ref, sh_ref, o_ref):
    o_ref[...] = (x_ref[...] * (1.0 + sc_ref[...]) + sh_ref[...]).astype(o_ref.dtype)


def _modulate(x, sc, sh, out_dtype):
    g, r, d = x.shape
    tr = min(r, 512)
    per_row = sc.shape[1] == r and r > 1
    return pl.pallas_call(
        _modulate_kernel,
        out_shape=jax.ShapeDtypeStruct((g, r, d), out_dtype),
        grid=(g, r // tr),
        in_specs=[pl.BlockSpec((None, tr, d), lambda g, i: (g, i, 0)),
                  _row_spec(tr, d, per_row), _row_spec(tr, d, per_row)],
        out_specs=pl.BlockSpec((None, tr, d), lambda g, i: (g, i, 0)),
        compiler_params=_params(2),
        name="modulate",
    )(x, sc, sh)


def _ln_kernel(alpha, has_mod, x_ref, y_ref, gt_ref, g_ref, b_ref, *rest):
    if has_mod:
        sc_ref, sh_ref, xo_ref, ho_ref, hr_ref = rest
    else:
        (xo_ref,) = rest
    z = alpha * x_ref[...] + gt_ref[...] * y_ref[...]
    mu = jnp.mean(z, axis=-1, keepdims=True)
    dz = z - mu
    var = jnp.mean(dz * dz, axis=-1, keepdims=True)
    xn = dz * lax.rsqrt(var + LN_EPS) * g_ref[...] + b_ref[...]
    xo_ref[...] = xn
    if has_mod:
        h = xn * (1.0 + sc_ref[...]) + sh_ref[...]
        ho_ref[...] = h
        for t in range(ROW_TILES):
            hr_ref[:, t, :] = h[:, t * LANES:(t + 1) * LANES]


def _post_ln(alpha, x, y, gate, ln_g, ln_b, sc=None, sh=None):
    g, r, d = x.shape
    tr = min(r, 256)
    per_row = gate.shape[1] == r and r > 1
    has_mod = sc is not None
    tile = pl.BlockSpec((None, tr, d), lambda g, i: (g, i, 0))
    vec = pl.BlockSpec((1, d), lambda g, i: (0, 0))
    in_specs = [tile, tile, _row_spec(tr, d, per_row), vec, vec]
    args = [x, y, gate, ln_g.reshape(1, d), ln_b.reshape(1, d)]
    out_shape = [jax.ShapeDtypeStruct((g, r, d), F32)]
    out_specs = [tile]
    if has_mod:
        in_specs += [_row_spec(tr, d, per_row), _row_spec(tr, d, per_row)]
        args += [sc, sh]
        nt = r // tr
        out_shape += [jax.ShapeDtypeStruct((g, r, d), F32), jax.ShapeDtypeStruct((g * r, ROW_TILES, LANES), F32)]
        out_specs += [tile, pl.BlockSpec((tr, ROW_TILES, LANES), lambda g, i: (g * nt + i, 0, 0))]
    res = pl.pallas_call(
        functools.partial(_ln_kernel, alpha, has_mod),
        out_shape=out_shape,
        grid=(g, r // tr),
        in_specs=in_specs,
        out_specs=out_specs,
        compiler_params=_params(2),
        name="post_ln",
    )(*args)
    return res if has_mod else (res[0], None, None)


def _mm_kernel(has_gate, has_add, a_ref, w_ref, *rest):
    rest = list(rest)
    gate_ref = rest.pop(0) if has_gate else None
    add_ref = rest.pop(0) if has_add else None
    o_ref, wbf = rest

    @pl.when(pl.program_id(1) == 0)
    def _():
        wbf[...] = w_ref[...].astype(BF16)

    acc = jnp.dot(a_ref[...], wbf[...], preferred_element_type=F32)
    if has_gate:
        acc = acc * jax.nn.sigmoid(gate_ref[...])
    if has_add:
        acc = acc + add_ref[...]
    o_ref[...] = acc.astype(o_ref.dtype)


def _mm(a, w, layer, *, tn, out_dtype, gate=None, gate_col=0, add=None):
    m, k = a.shape
    n = w.shape[2]
    tm = min(m, 512)
    in_specs = [pl.BlockSpec((tm, k), lambda j, i: (i, 0)),
                pl.BlockSpec((None, k, tn), lambda j, i: (layer, 0, j))]
    args = [a, w]
    if gate is not None:
        goff = gate_col // tn
        in_specs.append(pl.BlockSpec((tm, tn), lambda j, i: (i, goff + j)))
        args.append(gate)
    if add is not None:
        in_specs.append(pl.BlockSpec((tm, tn), lambda j, i: (i, j)))
        args.append(add)
    return pl.pallas_call(
        functools.partial(_mm_kernel, gate is not None, add is not None),
        out_shape=jax.ShapeDtypeStruct((m, n), out_dtype),
        grid=(n // tn, m // tm),
        in_specs=in_specs,
        out_specs=pl.BlockSpec((tm, tn), lambda j, i: (i, j)),
        scratch_shapes=[pltpu.VMEM((k, tn), BF16)],
        compiler_params=_params(2),
        name="matmul",
    )(*args)


def _select_topk(score, lane, k, n_lanes):
    sel = jnp.zeros(score.shape, dtype=jnp.bool_)
    picks = []
    for _ in range(k):
        m = jnp.max(score, axis=-1, keepdims=True)
        cand = jnp.where((score == m) & (m > NEG_INF), lane, n_lanes)
        idx = jnp.min(cand, axis=-1, keepdims=True)
        pick = lane == idx
        sel = sel | pick
        score = jnp.where(pick, NEG_INF, score)
        picks.append(pick)
    return sel, picks


def _select_topk_rows(score, row, k, n_rows):
    sel = jnp.zeros(score.shape, dtype=jnp.bool_)
    for _ in range(k):
        m = jnp.max(score, axis=0, keepdims=True)
        cand = jnp.where((score == m) & (m > NEG_INF), row, n_rows)
        idx = jnp.min(cand, axis=0, keepdims=True)
        pick = row == idx
        sel = sel | pick
        score = jnp.where(pick, NEG_INF, score)
    return sel


def _moba_prompt_kernel(nb, q_ref, k_ref, v_ref, o_ref, kbf, vtb, kmean, sel_s, acc_s):
    i = pl.program_id(2)
    scale = HEAD_DIM ** -0.5

    @pl.when(i == 0)
    def _():
        kf = k_ref[...]
        kbf[...] = kf.reshape(nb // 2, 2 * MOBA_BLOCK, HEAD_DIM).astype(BF16)
        kmean[...] = jnp.mean(kf.reshape(nb, MOBA_BLOCK, HEAD_DIM), axis=1)
        for j in range(nb // 2):
            vtb[j] = v_ref[j * 2 * MOBA_BLOCK:(j + 1) * 2 * MOBA_BLOCK, :].T.astype(BF16)

    q = q_ref[...]
    qt = jnp.concatenate([q[:, :HEAD_DIM].T, q[:, HEAD_DIM:].T], axis=1)
    width = qt.shape[1]
    gate = jnp.dot(kmean[...], qt, precision=lax.Precision.HIGHEST, preferred_element_type=F32)
    blk = lax.broadcasted_iota(jnp.int32, gate.shape, 0)
    gate = jnp.where(blk < i, gate, NEG_INF)
    sel_s[...] = _select_topk_rows(gate, blk, MOBA_TOPK, nb).astype(F32)

    qtb = qt.astype(BF16)

    own_off = pl.multiple_of((i % 2) * MOBA_BLOCK, MOBA_BLOCK)
    k_own = kbf[i // 2, pl.ds(own_off, MOBA_BLOCK), :]
    v_own = v_ref[pl.ds(pl.multiple_of(i * MOBA_BLOCK, MOBA_BLOCK), MOBA_BLOCK), :].T.astype(BF16)
    s = jnp.dot(k_own, qtb, preferred_element_type=F32) * scale
    kpos = lax.broadcasted_iota(jnp.int32, s.shape, 0)
    qpos = lax.broadcasted_iota(jnp.int32, s.shape, 1) % MOBA_BLOCK
    s = jnp.where(kpos <= qpos, s, NEG_INF)
    m0 = jnp.max(s, axis=0, keepdims=True)
    p = jnp.exp(s - m0)
    l0 = jnp.sum(p, axis=0, keepdims=True)
    acc_s[...] = jnp.dot(v_own, p.astype(BF16), preferred_element_type=F32)

    def body(t, carry):
        m, l = carry
        picked = jnp.concatenate(
            [jnp.broadcast_to(sel_s[pl.ds(2 * t, 1), :], (MOBA_BLOCK, width)),
             jnp.broadcast_to(sel_s[pl.ds(2 * t + 1, 1), :], (MOBA_BLOCK, width))], axis=0)
        s = jnp.dot(kbf[t], qtb, preferred_element_type=F32) * scale
        s = jnp.where(picked > 0.0, s, NEG_INF)
        m_new = jnp.maximum(m, jnp.max(s, axis=0, keepdims=True))
        a = jnp.exp(m - m_new)
        p = jnp.exp(s - m_new)
        l = a * l + jnp.sum(p, axis=0, keepdims=True)
        acc_s[...] = a * acc_s[...] + jnp.dot(vtb[t], p.astype(BF16), preferred_element_type=F32)
        return m_new, l

    _, l = lax.fori_loop(0, (i + 1) // 2, body, (m0, l0))
    ot = acc_s[...] / l
    half = width // 2
    o_ref[...] = jnp.concatenate([ot[:, :half].T, ot[:, half:].T], axis=1).astype(o_ref.dtype)


def _moba_prompt(p_all, bsz, seq):
    nb = seq // MOBA_BLOCK
    qw = Q_PER_KV * HEAD_DIM
    return pl.pallas_call(
        functools.partial(_moba_prompt_kernel, nb),
        out_shape=jax.ShapeDtypeStruct((bsz * seq, ATTN_Q_W), BF16),
        grid=(bsz, KV_HEADS, nb),
        in_specs=[pl.BlockSpec((MOBA_BLOCK, qw), lambda b, g, i: (b * nb + i, OFF_QA // qw + g)),
                  pl.BlockSpec((seq, HEAD_DIM), lambda b, g, i: (b, OFF_KA // HEAD_DIM + g)),
                  pl.BlockSpec((seq, HEAD_DIM), lambda b, g, i: (b, OFF_VA // HEAD_DIM + g))],
        out_specs=pl.BlockSpec((MOBA_BLOCK, qw), lambda b, g, i: (b * nb + i, g)),
        scratch_shapes=[pltpu.VMEM((nb // 2, 2 * MOBA_BLOCK, HEAD_DIM), BF16),
                        pltpu.VMEM((nb // 2, HEAD_DIM, 2 * MOBA_BLOCK), BF16),
                        pltpu.VMEM((nb, HEAD_DIM), F32), pltpu.VMEM((nb, Q_PER_KV * MOBA_BLOCK), F32),
                        pltpu.VMEM((HEAD_DIM, Q_PER_KV * MOBA_BLOCK), F32)],
        compiler_params=_params(3),
        name="moba_prompt",
    )(p_all, p_all, p_all)


def _moba_sample_kernel(n_blocks, pages_per_step, pt_ref, q_ref, kn_ref, vn_ref, *rest):
    k_pages = rest[:pages_per_step]
    v_pages = rest[pages_per_step:2 * pages_per_step]
    o_ref, gate_s, m_s, l_s, o_s = rest[2 * pages_per_step:]
    c = pl.program_id(1)
    scale = HEAD_DIM ** -0.5
    page_rows = k_pages[0].shape[0]
    pages_per_block = MOBA_BLOCK // page_rows
    bps = pages_per_step // pages_per_block

    q = q_ref[...]
    qb = q.astype(BF16)
    head = lax.broadcasted_iota(jnp.int32, (N_HEADS, 1), 0)
    lane = lax.broadcasted_iota(jnp.int32, q.shape, 1)

    @pl.when(c == 0)
    def _():
        gate_s[...] = jnp.zeros_like(gate_s)
        m_s[...] = jnp.zeros_like(m_s)
        l_s[...] = jnp.zeros_like(l_s)

    s_parts, km_parts = [], []
    for g in range(KV_HEADS):
        cols = slice(g * HEAD_DIM, (g + 1) * HEAD_DIM)
        kg = jnp.concatenate([kp[:, cols] for kp in k_pages], axis=0)
        s_parts.append(lax.dot_general(qb, kg.astype(BF16), (((1,), (1,)), ((), ())),
                                       preferred_element_type=F32))
        km_parts.append(jnp.mean(kg.reshape(bps, MOBA_BLOCK, HEAD_DIM), axis=1))
    s_all = jnp.concatenate(s_parts, axis=0) * scale
    km_all = jnp.concatenate(km_parts, axis=0)
    gate_all = lax.dot_general(q, km_all, (((1,), (1,)), ((), ())),
                               precision=lax.Precision.HIGHEST, preferred_element_type=F32)
    gcol = lax.broadcasted_iota(jnp.int32, gate_all.shape, 1)
    gate_mine = jnp.where(gcol // bps == head // Q_PER_KV, gate_all, 0.0)

    def own_rows(x):
        out = x[:N_HEADS]
        for g in range(1, KV_HEADS):
            out = jnp.where(head // Q_PER_KV == g, x[g * N_HEADS:(g + 1) * N_HEADS], out)
        return out

    for b in range(bps):
        blk = c * bps + b
        sb = s_all[:, b * MOBA_BLOCK:(b + 1) * MOBA_BLOCK]
        mb = jnp.max(sb, axis=-1, keepdims=True)
        pb = jnp.exp(sb - mb)
        lb = jnp.sum(pb, axis=-1, keepdims=True)
        pb = pb.astype(BF16)
        ob = None
        for g in range(KV_HEADS):
            cols = slice(g * HEAD_DIM, (g + 1) * HEAD_DIM)
            pages = range(b * pages_per_block, (b + 1) * pages_per_block)
            vg = jnp.concatenate([v_pages[t][:, cols] for t in pages], axis=0).astype(BF16)
            og = jnp.dot(pb[g * N_HEADS:(g + 1) * N_HEADS], vg, preferred_element_type=F32)
            ob = og if ob is None else jnp.where(head // Q_PER_KV == g, og, ob)
        gt = jnp.sum(jnp.where(gcol % bps == b, gate_mine, 0.0), axis=-1, keepdims=True)
        here = lane == blk
        gate_s[...] = jnp.where(here, gt, gate_s[...])
        m_s[...] = jnp.where(here, own_rows(mb), m_s[...])
        l_s[...] = jnp.where(here, own_rows(lb), l_s[...])
        o_s[blk] = ob

    @pl.when(c == pl.num_programs(1) - 1)
    def _():
        gate = jnp.where(lane < n_blocks, gate_s[...], NEG_INF)
        sel, _ = _select_topk(gate, lane, MOBA_TOPK, LANES)
        s_self = jnp.sum(q * kn_ref[...], axis=-1, keepdims=True) * scale
        m_all = m_s[...]
        m_fin = jnp.maximum(jnp.max(jnp.where(sel, m_all, NEG_INF), axis=-1, keepdims=True), s_self)
        w = jnp.where(sel, jnp.exp(m_all - m_fin), 0.0)
        w_self = jnp.exp(s_self - m_fin)
        denom = jnp.sum(w * l_s[...], axis=-1, keepdims=True) + w_self
        num = w_self * vn_ref[...]
        for b in range(n_blocks):
            wb = jnp.sum(jnp.where(lane == b, w, 0.0), axis=-1, keepdims=True)
            num = num + wb * o_s[b]
        o_ref[...] = num / denom


def _moba_sample(q, k_new, v_new, cache_k, cache_v, page_table, layer):
    s_cnt = q.shape[0]
    page_rows = cache_k.shape[2]
    n_pages = page_table.shape[0] // s_cnt
    n_blocks = n_pages * page_rows // MOBA_BLOCK
    pages_per_step = min(8, n_pages)
    steps = n_pages // pages_per_step

    def page_spec(t):
        return pl.BlockSpec((None, None, page_rows, ATTN_KV_W),
                            lambda s, c, pt: (layer, pt[s * n_pages + c * pages_per_step + t], 0, 0))

    head_spec = pl.BlockSpec((None, N_HEADS, HEAD_DIM), lambda s, c, pt: (s, 0, 0))
    grid_spec = pltpu.PrefetchScalarGridSpec(
        num_scalar_prefetch=1,
        grid=(s_cnt, steps),
        in_specs=[head_spec, head_spec, head_spec]
                 + [page_spec(t) for t in range(pages_per_step)] * 2,
        out_specs=head_spec,
        scratch_shapes=[pltpu.VMEM((N_HEADS, LANES), F32), pltpu.VMEM((N_HEADS, LANES), F32),
                        pltpu.VMEM((N_HEADS, LANES), F32), pltpu.VMEM((n_blocks, N_HEADS, HEAD_DIM), F32)],
    )
    return pl.pallas_call(
        functools.partial(_moba_sample_kernel, n_blocks, pages_per_step),
        out_shape=jax.ShapeDtypeStruct((s_cnt, N_HEADS, HEAD_DIM), F32),
        grid_spec=grid_spec,
        compiler_params=_params(2),
        name="moba_sample",
    )(page_table, q, k_new, v_new, *([cache_k] * pages_per_step), *([cache_v] * pages_per_step))


def _group_norm_gate(o, gn, gr):
    mu = jnp.mean(o, axis=-1, keepdims=True)
    do = o - mu
    var = jnp.mean(do * do, axis=-1, keepdims=True)
    return do * lax.rsqrt(var + LN_EPS) * gn * _silu(gr)


def _ret_prompt_kernel(heads, q_ref, k_ref, v_ref, g_ref, cos_ref, sin_ref, dmask_ref, qdec_ref, kdec_ref, sdec_ref,
                       gn_ref, o_ref, s_ref):
    @pl.when(pl.program_id(2) == 0)
    def _():
        s_ref[...] = jnp.zeros_like(s_ref)

    cos = cos_ref[...]
    sin = sin_ref[...]
    half = RET_DK // 2

    def rot(x):
        x1, x2 = x[:, :half], x[:, half:]
        return jnp.concatenate([x1 * cos - x2 * sin, x2 * cos + x1 * sin], axis=1)

    for h in range(heads):
        kc = slice(h * RET_DK, (h + 1) * RET_DK)
        vc = slice(h * RET_DV, (h + 1) * RET_DV)
        q = rot(q_ref[:, kc])
        k = rot(k_ref[:, kc]) * (RET_DK ** -0.5)
        qb = q.astype(BF16)
        vb = v_ref[:, vc].astype(BF16)
        state = s_ref[h]
        scores = lax.dot_general(qb, k.astype(BF16), (((1,), (1,)), ((), ())),
                                 preferred_element_type=F32) * dmask_ref[h]
        o = jnp.dot(scores.astype(BF16), vb, preferred_element_type=F32)
        o = o + jnp.dot(qb, state.astype(BF16), preferred_element_type=F32) * qdec_ref[h]
        kd = (k * kdec_ref[h]).astype(BF16)
        s_ref[h] = state * sdec_ref[h] + lax.dot_general(kd, vb, (((0,), (0,)), ((), ())),
                                                         preferred_element_type=F32)
        o_ref[:, vc] = _group_norm_gate(o, gn_ref[:, vc], g_ref[:, vc]).astype(o_ref.dtype)


def _decay_tables(c):
    lg = jnp.log1p(-jnp.exp2(-5.0 - jnp.arange(RET_HEADS, dtype=F32)))
    i = jnp.arange(c, dtype=F32)
    diff = i[:, None] - i[None, :]
    causal = diff >= 0.0
    dmask = jnp.where(causal[None], jnp.exp(jnp.where(causal, diff, 0.0)[None] * lg[:, None, None]), 0.0)
    qdec = jnp.exp((i[:, None] + 1.0) * lg[None, :]).T[:, :, None]
    kdec = jnp.exp((c - 1.0 - i)[:, None] * lg[None, :]).T[:, :, None]
    sdec = jnp.exp(c * lg)
    return dmask, qdec, kdec, sdec


def _rot_tables(pos):
    half = RET_DK // 2
    theta = ROT_BASE ** -jnp.linspace(0.0, 1.0, half, dtype=F32)
    ang = pos.astype(F32)[:, None] * theta[None, :]
    return jnp.cos(ang), jnp.sin(ang)


def _ret_prompt(p_all, bsz, seq, gn):
    c = min(RET_CHUNK, seq)
    nc = seq // c
    hps = 4
    cos, sin = _rot_tables(jnp.arange(seq, dtype=jnp.int32))
    dmask, qdec, kdec, sdec = _decay_tables(c)
    kw, vw = hps * RET_DK, hps * RET_DV
    head_vec = lambda b, h, j: (h, 0, 0)
    return pl.pallas_call(
        functools.partial(_ret_prompt_kernel, hps),
        out_shape=[jax.ShapeDtypeStruct((bsz * seq, RET_V_W), BF16),
                   jax.ShapeDtypeStruct((bsz, RET_HEADS, RET_DK, RET_DV), F32)],
        grid=(bsz, RET_HEADS // hps, nc),
        in_specs=[pl.BlockSpec((c, kw), lambda b, h, j: (b * nc + j, OFF_QR // kw + h)),
                  pl.BlockSpec((c, kw), lambda b, h, j: (b * nc + j, OFF_KR // kw + h)),
                  pl.BlockSpec((c, vw), lambda b, h, j: (b * nc + j, OFF_VR // vw + h)),
                  pl.BlockSpec((c, vw), lambda b, h, j: (b * nc + j, OFF_GR // vw + h)),
                  pl.BlockSpec((c, RET_DK // 2), lambda b, h, j: (j, 0)),
                  pl.BlockSpec((c, RET_DK // 2), lambda b, h, j: (j, 0)),
                  pl.BlockSpec((hps, c, c), head_vec),
                  pl.BlockSpec((hps, c, 1), head_vec),
                  pl.BlockSpec((hps, c, 1), head_vec),
                  pl.BlockSpec((hps, 1, 1), head_vec),
                  pl.BlockSpec((1, vw), lambda b, h, j: (0, h))],
        out_specs=[pl.BlockSpec((c, vw), lambda b, h, j: (b * nc + j, h)),
                   pl.BlockSpec((None, hps, RET_DK, RET_DV), lambda b, h, j: (b, h, 0, 0))],
        compiler_params=_params(3),
        name="retention_prompt",
    )(p_all, p_all, p_all, p_all, cos, sin, dmask, qdec, kdec, sdec.reshape(RET_HEADS, 1, 1),
      gn.reshape(1, RET_V_W))


def _ret_sample_kernel(gam_ref, q_ref, k_ref, v_ref, g_ref, cos_ref, sin_ref, gn_ref, s_in, *rest):
    o_ref, s_out = rest[-2:]
    cos = cos_ref[...]
    sin = sin_ref[...]
    half = RET_DK // 2

    def rot(x):
        x1, x2 = x[:half], x[half:]
        return jnp.concatenate([x1 * cos - x2 * sin, x2 * cos + x1 * sin], axis=0)

    qc = rot(q_ref[...])
    kc = rot(k_ref[...]) * (RET_DK ** -0.5)
    v = v_ref[...]
    gr = g_ref[...]
    gn = gn_ref[...]
    for h in range(RET_HEADS):
        cols = slice(h * RET_DV, (h + 1) * RET_DV)
        new_state = s_in[h] * gam_ref[h] + kc[:, h:h + 1] * v[:, cols]
        s_out[h] = new_state
        o = jnp.sum(qc[:, h:h + 1] * new_state, axis=0, keepdims=True)
        o_ref[:, cols] = _group_norm_gate(o, gn[:, cols], gr[:, cols])


def _ret_sample(q_t, k_t, v, g, state_ret, layer, pos, gn, prev_out):
    s_cnt = q_t.shape[0]
    cos, sin = _rot_tables(jnp.full((1,), pos, dtype=jnp.int32))
    cos, sin = cos.reshape(-1, 1), sin.reshape(-1, 1)
    gam = _decay_tables(1)[3]
    col_spec = pl.BlockSpec((None, RET_DK, RET_HEADS), lambda s: (s, 0, 0))
    row_spec = pl.BlockSpec((None, 1, RET_V_W), lambda s: (s, 0, 0))
    rot_spec = pl.BlockSpec((RET_DK // 2, 1), lambda s: (0, 0))
    state_spec = pl.BlockSpec((None, None, RET_HEADS, RET_DK, RET_DV), lambda s: (layer, s, 0, 0, 0))
    in_specs = [pl.BlockSpec(memory_space=pltpu.SMEM), col_spec, col_spec, row_spec, row_spec, rot_spec, rot_spec,
                pl.BlockSpec((1, RET_V_W), lambda s: (0, 0)), state_spec]
    args = [gam, q_t, k_t, v, g, cos, sin, gn.reshape(1, RET_V_W), state_ret]
    aliases = {}
    if prev_out is not None:
        in_specs.append(pl.BlockSpec(memory_space=pl.ANY))
        args.append(prev_out)
        aliases = {len(args) - 1: 1}
    return pl.pallas_call(
        _ret_sample_kernel,
        out_shape=[jax.ShapeDtypeStruct((s_cnt, 1, RET_V_W), F32),
                   jax.ShapeDtypeStruct(state_ret.shape, state_ret.dtype)],
        grid=(s_cnt,),
        in_specs=in_specs,
        out_specs=[row_spec, state_spec],
        input_output_aliases=aliases,
        compiler_params=_params(1),
        name="retention_sample",
    )(*args)


def _router_kernel(h_ref, w_ref, b_ref, idx_ref, wt_ref):
    logits = jnp.dot(h_ref[...], w_ref[...], precision=lax.Precision.HIGHEST, preferred_element_type=F32)
    aff = jax.nn.sigmoid(logits)
    lane = lax.broadcasted_iota(jnp.int32, aff.shape, 1)
    _, picks = _select_topk(aff + b_ref[...], lane, TOP_K, N_EXPERTS)
    out_lane = lax.broadcasted_iota(jnp.int32, idx_ref.shape, 1)
    idx = jnp.zeros(idx_ref.shape, jnp.int32)
    top_aff = jnp.zeros(idx_ref.shape, F32)
    for t, pick in enumerate(picks):
        e_t = jnp.sum(jnp.where(pick, lane, 0), axis=-1, keepdims=True)
        a_t = jnp.sum(jnp.where(pick, aff, 0.0), axis=-1, keepdims=True)
        idx = jnp.where(out_lane == t, e_t, idx)
        top_aff = jnp.where(out_lane == t, a_t, top_aff)
    idx_ref[...] = idx
    wt_ref[...] = top_aff / jnp.sum(top_aff, axis=-1, keepdims=True) * ROUTED_SCALE


def _router(h, w_router_l, bias_l):
    m, d = h.shape
    tm = min(m, 512)
    out = pl.BlockSpec((tm, LANES), lambda i: (i, 0))
    return pl.pallas_call(
        _router_kernel,
        out_shape=[jax.ShapeDtypeStruct((m, LANES), jnp.int32), jax.ShapeDtypeStruct((m, LANES), F32)],
        grid=(m // tm,),
        in_specs=[pl.BlockSpec((tm, d), lambda i: (i, 0)),
                  pl.BlockSpec((d, N_EXPERTS), lambda i: (0, 0)),
                  pl.BlockSpec((1, N_EXPERTS), lambda i: (0, 0))],
        out_specs=[out, out],
        compiler_params=_params(1),
        name="router",
    )(h, w_router_l, bias_l.reshape(1, N_EXPERTS))


def _swiglu_bf16(x, wg, wu, wd):
    g = jnp.dot(x, wg, preferred_element_type=F32)
    u = jnp.dot(x, wu, preferred_element_type=F32)
    return jnp.dot((_silu(g) * u).astype(BF16), wd, preferred_element_type=F32)


def _shared_kernel(h_ref, wg_ref, wu_ref, wd_ref, o_ref, wgb, wub, wdb):
    @pl.when(pl.program_id(0) == 0)
    def _():
        wgb[...] = wg_ref[...].astype(BF16)
        wub[...] = wu_ref[...].astype(BF16)
        wdb[...] = wd_ref[...].astype(BF16)

    o_ref[...] = _swiglu_bf16(h_ref[...].astype(BF16), wgb[...], wub[...], wdb[...])


def _shared_expert(h, wg, wu, wd, layer):
    m, d = h.shape
    tm = min(m, 512)
    return pl.pallas_call(
        _shared_kernel,
        out_shape=jax.ShapeDtypeStruct((m, d), F32),
        grid=(m // tm,),
        in_specs=[pl.BlockSpec((tm, d), lambda i: (i, 0)),
                  pl.BlockSpec((None, d, D_SHARED), lambda i: (layer, 0, 0)),
                  pl.BlockSpec((None, d, D_SHARED), lambda i: (layer, 0, 0)),
                  pl.BlockSpec((None, D_SHARED, d), lambda i: (layer, 0, 0))],
        out_specs=pl.BlockSpec((tm, d), lambda i: (i, 0)),
        scratch_shapes=[pltpu.VMEM((d, D_SHARED), BF16), pltpu.VMEM((d, D_SHARED), BF16),
                        pltpu.VMEM((D_SHARED, d), BF16)],
        compiler_params=_params(1),
        name="shared_expert",
    )(h, wg, wu, wd)


def _experts_kernel(blk_e, src, dst, n_used, h_hbm, wrow_ref, wg_ref, wu_ref, wd_ref, y_hbm,
                    xs, ys, wgb, wub, wdb, gsem, ssem):
    i = pl.program_id(0)
    n_act = n_used[0]
    last = pl.num_programs(0) - 1
    groups = MOE_ROWS // SUBLANES

    def gather(blk, slot, r, src_row):
        return pltpu.make_async_copy(h_hbm.at[src_row], xs.at[slot, r // SUBLANES, :, r % SUBLANES, :],
                                     gsem.at[slot])

    def scatter(blk, slot, r, dst_row):
        return pltpu.make_async_copy(ys.at[slot, r // SUBLANES, :, r % SUBLANES, :], y_hbm.at[dst_row],
                                     ssem.at[slot])

    def start_gathers(blk, slot):
        for r in range(MOE_ROWS):
            gather(blk, slot, r, src[blk * MOE_ROWS + r]).start()

    def wait_gathers(slot):
        for r in range(MOE_ROWS):
            gather(0, slot, r, 0).wait()

    def start_scatters(blk, slot):
        for r in range(MOE_ROWS):
            scatter(blk, slot, r, dst[blk * MOE_ROWS + r]).start()

    def wait_scatters(slot):
        for r in range(MOE_ROWS):
            scatter(0, slot, r, 0).wait()

    @pl.when(i < n_act)
    def _():
        slot = i % 2
        nxt = jnp.minimum(i + 1, last)

        @pl.when(i == 0)
        def _():
            start_gathers(0, 0)

        @pl.when((i == 0) | (blk_e[i] != blk_e[jnp.maximum(i - 1, 0)]))
        def _():
            wgb[...] = wg_ref[...].astype(BF16)
            wub[...] = wu_ref[...].astype(BF16)
            wdb[...] = wd_ref[...].astype(BF16)

        wait_gathers(slot)

        @pl.when(i >= 2)
        def _():
            wait_scatters(slot)

        start_gathers(nxt, 1 - slot)
        x = jnp.concatenate([xs[slot, :, t].reshape(MOE_ROWS, LANES) for t in range(ROW_TILES)], axis=1)
        y = _swiglu_bf16(x.astype(BF16), wgb[...], wub[...], wdb[...]) * wrow_ref[...]
        for t in range(ROW_TILES):
            ys[slot, :, t] = y[:, t * LANES:(t + 1) * LANES].reshape(groups, SUBLANES, LANES)
        start_scatters(i, slot)

        @pl.when(i == n_act - 1)
        def _():
            wait_gathers(1 - slot)
            wait_scatters(slot)

            @pl.when(i >= 1)
            def _():
                wait_scatters(1 - slot)


def _routed_experts(h_rows, top_idx, top_w, wg, wu, wd, layer, n_out_rows):
    m = h_rows.shape[0]
    a = m * TOP_K
    n_blk = -(-(a + N_EXPERTS * (MOE_ROWS - 1)) // MOE_ROWS)
    n_rows = n_blk * MOE_ROWS
    flat_e = top_idx.reshape(a)
    order = jnp.argsort(flat_e).astype(jnp.int32)
    e_sorted = flat_e[order]
    experts = jnp.arange(N_EXPERTS, dtype=jnp.int32)
    start = jnp.searchsorted(e_sorted, experts, side='left').astype(jnp.int32)
    counts = jnp.searchsorted(e_sorted, experts, side='right').astype(jnp.int32) - start
    padded = (counts + MOE_ROWS - 1) // MOE_ROWS * MOE_ROWS
    ends = jnp.cumsum(padded)
    pstart = ends - padded
    blk_e = jnp.minimum(jnp.searchsorted(ends, jnp.arange(n_blk, dtype=jnp.int32) * MOE_ROWS, side='right'),
                        N_EXPERTS - 1).astype(jnp.int32)
    n_used = (ends[-1:] // MOE_ROWS).astype(jnp.int32)
    pos = jnp.arange(n_rows, dtype=jnp.int32)
    e_p = jnp.repeat(blk_e, MOE_ROWS)
    rank = pos - pstart[e_p]
    valid = rank < counts[e_p]
    asg = order[jnp.clip(start[e_p] + rank, 0, a - 1)]
    src = jnp.where(valid, asg // TOP_K, 0)
    dst = jnp.where(valid, asg, a + pos % (2 * MOE_ROWS))
    wrow = jnp.where(valid, top_w.reshape(a)[asg], 0.0).reshape(n_rows, 1)

    d = D_MODEL
    groups = MOE_ROWS // SUBLANES
    wspec = lambda i, be, s, t, n: (layer, be[i], 0, 0)
    grid_spec = pltpu.PrefetchScalarGridSpec(
        num_scalar_prefetch=4,
        grid=(n_blk,),
        in_specs=[pl.BlockSpec(memory_space=pl.ANY),
                  pl.BlockSpec((MOE_ROWS, 1), lambda i, be, s, t, n: (i, 0)),
                  pl.BlockSpec((None, None, d, D_EXPERT), wspec),
                  pl.BlockSpec((None, None, d, D_EXPERT), wspec),
                  pl.BlockSpec((None, None, D_EXPERT, d), wspec)],
        out_specs=pl.BlockSpec(memory_space=pl.ANY),
        scratch_shapes=[pltpu.VMEM((2, groups, ROW_TILES, SUBLANES, LANES), F32),
                        pltpu.VMEM((2, groups, ROW_TILES, SUBLANES, LANES), F32),
                        pltpu.VMEM((d, D_EXPERT), BF16), pltpu.VMEM((d, D_EXPERT), BF16),
                        pltpu.VMEM((D_EXPERT, d), BF16),
                        pltpu.SemaphoreType.DMA((2,)), pltpu.SemaphoreType.DMA((2,))],
    )
    return pl.pallas_call(
        _experts_kernel,
        out_shape=jax.ShapeDtypeStruct((n_out_rows, ROW_TILES, LANES), F32),
        grid_spec=grid_spec,
        compiler_params=_params(1),
        name="routed_experts",
    )(blk_e, src, dst, n_used, h_rows, wrow, wg, wu, wd)


def _combine_kernel(alpha, has_mod, y_ref, sh_ref, x_ref, gt_ref, g_ref, b_ref, *rest):
    if has_mod:
        sc_ref, shift_ref, xo_ref, ho_ref, slab_s, z_s = rest
    else:
        xo_ref, slab_s, z_s = rest
    routed = y_ref[:, 0]
    for k in range(1, TOP_K):
        routed = routed + y_ref[:, k]
    slab_s[...] = routed
    for t in range(ROW_TILES):
        cols = slice(t * LANES, (t + 1) * LANES)
        z_s[:, cols] = sh_ref[:, cols] + slab_s[:, t, :]
    z = alpha * x_ref[...] + gt_ref[...] * z_s[...]
    mu = jnp.mean(z, axis=-1, keepdims=True)
    dz = z - mu
    var = jnp.mean(dz * dz, axis=-1, keepdims=True)
    xn = dz * lax.rsqrt(var + LN_EPS) * g_ref[...] + b_ref[...]
    xo_ref[...] = xn
    if has_mod:
        ho_ref[...] = (xn * (1.0 + sc_ref[...]) + shift_ref[...]).astype(ho_ref.dtype)


def _combine_ln(alpha, y_rows, shared, x, gate, ln_g, ln_b, tok0, sc=None, sh=None):
    g, r, d = x.shape
    tr = min(r, 128)
    per_row = gate.shape[1] == r and r > 1
    has_mod = sc is not None
    nt = r // tr
    t0 = tok0 // tr
    tile = pl.BlockSpec((None, tr, d), lambda g, i: (g, i, 0))
    flat = pl.BlockSpec((tr, d), lambda g, i: (g * nt + i, 0))
    vec = pl.BlockSpec((1, d), lambda g, i: (0, 0))
    in_specs = [pl.BlockSpec((tr, TOP_K, ROW_TILES, LANES), lambda g, i: (t0 + g * nt + i, 0, 0, 0)),
                flat, tile, _row_spec(tr, d, per_row), vec, vec]
    args = [y_rows, shared, x, gate, ln_g.reshape(1, d), ln_b.reshape(1, d)]
    out_shape = [jax.ShapeDtypeStruct((g, r, d), F32)]
    out_specs = [tile]
    if has_mod:
        in_specs += [_row_spec(tr, d, per_row), _row_spec(tr, d, per_row)]
        args += [sc, sh]
        out_shape.append(jax.ShapeDtypeStruct((g, r, d), BF16))
        out_specs.append(tile)
    res = pl.pallas_call(
        functools.partial(_combine_kernel, alpha, has_mod),
        out_shape=out_shape,
        grid=(g, nt),
        in_specs=in_specs,
        out_specs=out_specs,
        scratch_shapes=[pltpu.VMEM((tr, ROW_TILES, LANES), F32), pltpu.VMEM((tr, d), F32)],
        compiler_params=_params(2),
        name="moe_combine_ln",
    )(*args)
    return res if has_mod else (res[0], None)


def kernel(x_prompt, x_sample, cache_k, cache_v, state_ret, page_table, c_prompt, c_sample, w_ada, b_ada, w_in, w_attn_br, ret_norm_g, w_ret_br, w_o, ln_g, ln_b, w_router, router_bias, w_exp_gate, w_exp_up, w_exp_down, w_sh_gate, w_sh_up, w_sh_down):
    bsz, seq, d = x_prompt.shape
    s_cnt = x_sample.shape[0]
    depth = w_in.shape[0]
    alpha = (2.0 * depth) ** 0.25
    n_phys, page_rows = cache_k.shape[1], cache_k.shape[2]
    past = page_table.shape[1] * page_rows
    m_p = bsz * seq
    m_all = m_p + s_cnt
    y_tokens = m_all + -(-2 * MOE_ROWS // TOP_K)

    n_c = bsz + s_cnt
    r_c = -(-n_c // SUBLANES) * SUBLANES
    c_all = jnp.concatenate([c_prompt, c_sample, jnp.zeros((r_c - n_c, d), F32)], axis=0)
    mod = _adaln(c_all, w_ada, b_ada)
    mod_p = mod[:, :bsz].reshape(depth, bsz, N_MOD, 1, d)
    mod_s = mod[:, bsz:n_c].reshape(depth, 1, s_cnt, N_MOD, d)

    def mods(layer):
        mp = [mod_p[layer, :, j] for j in range(N_MOD)]
        ms = [mod_s[layer, :, :, j] for j in range(N_MOD)]
        return mp, ms

    cache_k2 = cache_k.reshape(depth, n_phys, page_rows, ATTN_KV_W)
    cache_v2 = cache_v.reshape(depth, n_phys, page_rows, ATTN_KV_W)
    pt_flat = page_table.reshape(-1)

    xp = x_prompt
    xs = x_sample.reshape(1, s_cnt, d)
    mp, ms = mods(0)
    hp = _modulate(xp, mp[1], mp[0], BF16)
    hs = _modulate(xs, ms[1], ms[0], BF16)

    kp, vp, sp, kd, vd = [], [], [], [], []
    state_out = None
    for layer in range(depth):
        mp, ms = mods(layer)
        pp = _mm(hp.reshape(m_p, d), w_in, layer, tn=1024, out_dtype=F32)
        o_a = _moba_prompt(pp, bsz, seq)
        o_r, s_new = _ret_prompt(pp, bsz, seq, ret_norm_g[layer])
        t1 = _mm(o_a, w_attn_br, layer, tn=512, out_dtype=F32, gate=pp, gate_col=OFF_BA)
        mixed = _mm(o_r, w_ret_br, layer, tn=512, out_dtype=BF16, gate=pp, gate_col=OFF_BR, add=t1)
        y_p = _mm(mixed, w_o, layer, tn=512, out_dtype=F32).reshape(bsz, seq, d)
        xp, h2p, hr_p = _post_ln(alpha, xp, y_p, mp[2], ln_g[layer, 0], ln_b[layer, 0], mp[4], mp[3])
        kp.append(pp[:, OFF_KA:OFF_VA].reshape(bsz, seq, KV_HEADS, HEAD_DIM))
        vp.append(pp[:, OFF_VA:OFF_QR].reshape(bsz, seq, KV_HEADS, HEAD_DIM))
        sp.append(s_new)

        ps = _mm(hs.reshape(s_cnt, d), w_in, layer, tn=1024, out_dtype=F32)
        k_new = ps[:, OFF_KA:OFF_VA].reshape(s_cnt, KV_HEADS, HEAD_DIM)
        v_new = ps[:, OFF_VA:OFF_QR].reshape(s_cnt, KV_HEADS, HEAD_DIM)
        o_as = _moba_sample(ps[:, OFF_QA:OFF_KA].reshape(s_cnt, N_HEADS, HEAD_DIM),
                            jnp.repeat(k_new, Q_PER_KV, axis=1), jnp.repeat(v_new, Q_PER_KV, axis=1),
                            cache_k2, cache_v2, pt_flat, layer)
        q_t = ps[:, OFF_QR:OFF_KR].reshape(s_cnt, RET_HEADS, RET_DK).transpose(0, 2, 1)
        k_t = ps[:, OFF_KR:OFF_VR].reshape(s_cnt, RET_HEADS, RET_DK).transpose(0, 2, 1)
        o_rs, state_out = _ret_sample(q_t, k_t, ps[:, OFF_VR:OFF_GR].reshape(s_cnt, 1, RET_V_W),
                                      ps[:, OFF_GR:OFF_BA].reshape(s_cnt, 1, RET_V_W),
                                      state_ret, layer, past, ret_norm_g[layer], state_out)
        t1s = _mm(o_as.reshape(s_cnt, ATTN_Q_W).astype(BF16), w_attn_br, layer, tn=512, out_dtype=F32,
                  gate=ps, gate_col=OFF_BA)
        mixed_s = _mm(o_rs.reshape(s_cnt, RET_V_W).astype(BF16), w_ret_br, layer, tn=512, out_dtype=BF16,
                      gate=ps, gate_col=OFF_BR, add=t1s)
        y_s = _mm(mixed_s, w_o, layer, tn=512, out_dtype=F32).reshape(1, s_cnt, d)
        xs, h2s, hr_s = _post_ln(alpha, xs, y_s, ms[2], ln_g[layer, 0], ln_b[layer, 0], ms[4], ms[3])
        kd.append(k_new.reshape(s_cnt, 1, KV_HEADS, HEAD_DIM))
        vd.append(v_new.reshape(s_cnt, 1, KV_HEADS, HEAD_DIM))

        h2p = h2p.reshape(m_p, d)
        h2s = h2s.reshape(s_cnt, d)
        idx_p, wt_p = _router(h2p, w_router[layer], router_bias[layer])
        idx_s, wt_s = _router(h2s, w_router[layer], router_bias[layer])
        sh_p = _shared_expert(h2p, w_sh_gate, w_sh_up, w_sh_down, layer)
        sh_s = _shared_expert(h2s, w_sh_gate, w_sh_up, w_sh_down, layer)
        h_rows = jnp.concatenate([hr_p, hr_s], axis=0)
        top_idx = jnp.concatenate([idx_p[:, :TOP_K], idx_s[:, :TOP_K]], axis=0)
        top_w = jnp.concatenate([wt_p[:, :TOP_K], wt_s[:, :TOP_K]], axis=0)
        y_rows = _routed_experts(h_rows, top_idx, top_w, w_exp_gate, w_exp_up, w_exp_down, layer,
                                 y_tokens * TOP_K)
        y_rows = y_rows.reshape(y_tokens, TOP_K, ROW_TILES, LANES)
        if layer + 1 < depth:
            mp_n, ms_n = mods(layer + 1)
            nxt_p, nxt_s = (mp_n[1], mp_n[0]), (ms_n[1], ms_n[0])
        else:
            nxt_p = nxt_s = (None, None)
        xp, hp = _combine_ln(alpha, y_rows, sh_p, xp, mp[5], ln_g[layer, 1], ln_b[layer, 1], 0, *nxt_p)
        xs, hs = _combine_ln(alpha, y_rows, sh_s, xs, ms[5], ln_g[layer, 1], ln_b[layer, 1], m_p, *nxt_s)

    return (xp, xs.reshape(s_cnt, 1, d), jnp.stack(kp), jnp.stack(vp), jnp.stack(sp).astype(state_ret.dtype),
            jnp.stack(kd), jnp.stack(vd), state_out)
```

```python
import functools

import jax
import jax.numpy as jnp
import numpy as np
from jax import lax
from jax.experimental import pallas as pl
from jax.experimental.pallas import tpu as pltpu

F32 = jnp.float32
BF16 = jnp.bfloat16

D_MODEL = 2048
N_HEADS = 16
KV_HEADS = 8
HEAD_DIM = 128
Q_PER_KV = N_HEADS // KV_HEADS
MOBA_BLOCK = 256
MOBA_TOPK = 3
RET_HEADS = 8
RET_DK = D_MODEL // RET_HEADS
RET_DV = 2 * D_MODEL // RET_HEADS
RET_CHUNK = 128
ROT_BASE = 10000.0
N_EXPERTS = 64
TOP_K = 6
D_EXPERT = D_MODEL // 4
D_SHARED = D_MODEL // 4
ROUTED_SCALE = 2.5
LN_EPS = 1e-5
N_MOD = 6

ATTN_Q_W = N_HEADS * HEAD_DIM
ATTN_KV_W = KV_HEADS * HEAD_DIM
RET_QK_W = RET_HEADS * RET_DK
RET_V_W = RET_HEADS * RET_DV
OFF_QA = 0
OFF_KA = OFF_QA + ATTN_Q_W
OFF_VA = OFF_KA + ATTN_KV_W
OFF_QR = OFF_VA + ATTN_KV_W
OFF_KR = OFF_QR + RET_QK_W
OFF_VR = OFF_KR + RET_QK_W
OFF_GR = OFF_VR + RET_V_W
OFF_BA = OFF_GR + RET_V_W
OFF_BR = OFF_BA + D_MODEL
D_IN = OFF_BR + D_MODEL

LANES = 128
SUBLANES = 8
VMEM_LIMIT_BYTES = 52 * 1024 * 1024
MOE_ROWS = 256
ROW_TILES = D_MODEL // LANES

NEG_INF = float("-inf")


def _params(n_axes):
    return pltpu.CompilerParams(dimension_semantics=("arbitrary",) * n_axes,
                                vmem_limit_bytes=VMEM_LIMIT_BYTES)


def _silu(x):
    return x * jax.nn.sigmoid(x)


def _ada_kernel(c_ref, w_ref, b_ref, o_ref):
    a = _silu(c_ref[...]).astype(BF16)
    o_ref[...] = jnp.dot(a, w_ref[...].astype(BF16), preferred_element_type=F32) + b_ref[...]


def _adaln(c_all, w_ada, b_ada):
    depth, d, n = w_ada.shape
    r = c_all.shape[0]
    tn = 1024
    return pl.pallas_call(
        _ada_kernel,
        out_shape=jax.ShapeDtypeStruct((depth, r, n), F32),
        grid=(depth, n // tn),
        in_specs=[pl.BlockSpec((r, d), lambda l, j: (0, 0)),
                  pl.BlockSpec((None, d, tn), lambda l, j: (l, 0, j)),
                  pl.BlockSpec((None, 1, tn), lambda l, j: (l, 0, j))],
        out_specs=pl.BlockSpec((None, r, tn), lambda l, j: (l, 0, j)),
        compiler_params=_params(2),
        name="adaln",
    )(c_all, w_ada, b_ada.reshape(depth, 1, n))


def _row_spec(tr, d, per_row):
    if per_row:
        return pl.BlockSpec((None, tr, d), lambda g, i: (g, i, 0))
    return pl.BlockSpec((None, 1, d), lambda g, i: (g, 0, 0))


def _modulate_kernel(x_ref, sc_ref, sh_ref, o_ref):
    o_ref[...] = (x_ref[...] * (1.0 + sc_ref[...]) + sh_ref[...]).astype(o_ref.dtype)


def _modulate(x, sc, sh, out_dtype):
    g, r, d = x.shape
    tr = min(r, 512)
    per_row = sc.shape[1] == r and r > 1
    return pl.pallas_call(
        _modulate_kernel,
        out_shape=jax.ShapeDtypeStruct((g, r, d), out_dtype),
        grid=(g, r // tr),
        in_specs=[pl.BlockSpec((None, tr, d), lambda g, i: (g, i, 0)),
                  _row_spec(tr, d, per_row), _row_spec(tr, d, per_row)],
        out_specs=pl.BlockSpec((None, tr, d), lambda g, i: (g, i, 0)),
        compiler_params=_params(2),
        name="modulate",
    )(x, sc, sh)


def _ln_kernel(alpha, has_mod, x_ref, y_ref, gt_ref, g_ref, b_ref, *rest):
    if has_mod:
        sc_ref, sh_ref, xo_ref, ho_ref, hr_ref = rest
    else:
        (xo_ref,) = rest
    z = alpha * x_ref[...] + gt_ref[...] * y_ref[...]
    mu = jnp.mean(z, axis=-1, keepdims=True)
    dz = z - mu
    var = jnp.mean(dz * dz, axis=-1, keepdims=True)
    xn = dz * lax.rsqrt(var + LN_EPS) * g_ref[...] + b_ref[...]
    xo_ref[...] = xn
    if has_mod:
        h = xn * (1.0 + sc_ref[...]) + sh_ref[...]
        ho_ref[...] = h
        for t in range(ROW_TILES):
            hr_ref[:, t, :] = h[:, t * LANES:(t + 1) * LANES]


def _post_ln(alpha, x, y, gate, ln_g, ln_b, sc=None, sh=None):
    g, r, d = x.shape
    tr = min(r, 256)
    per_row = gate.shape[1] == r and r > 1
    has_mod = sc is not None
    tile = pl.BlockSpec((None, tr, d), lambda g, i: (g, i, 0))
    vec = pl.BlockSpec((1, d), lambda g, i: (0, 0))
    in_specs = [tile, tile, _row_spec(tr, d, per_row), vec, vec]
    args = [x, y, gate, ln_g.reshape(1, d), ln_b.reshape(1, d)]
    out_shape = [jax.ShapeDtypeStruct((g, r, d), F32)]
    out_specs = [tile]
    if has_mod:
        in_specs += [_row_spec(tr, d, per_row), _row_spec(tr, d, per_row)]
        args += [sc, sh]
        nt = r // tr
        out_shape += [jax.ShapeDtypeStruct((g, r, d), F32), jax.ShapeDtypeStruct((g * r, ROW_TILES, LANES), F32)]
        out_specs += [tile, pl.BlockSpec((tr, ROW_TILES, LANES), lambda g, i: (g * nt + i, 0, 0))]
    res = pl.pallas_call(
        functools.partial(_ln_kernel, alpha, has_mod),
        out_shape=out_shape,
        grid=(g, r // tr),
        in_specs=in_specs,
        out_specs=out_specs,
        compiler_params=_params(2),
        name="post_ln",
    )(*args)
    return res if has_mod else (res[0], None, None)


def _mm_kernel(has_gate, has_add, a_ref, w_ref, *rest):
    rest = list(rest)
    gate_ref = rest.pop(0) if has_gate else None
    add_ref = rest.pop(0) if has_add else None
    o_ref, wbf = rest

    @pl.when(pl.program_id(1) == 0)
    def _():
        wbf[...] = w_ref[...].astype(BF16)

    acc = jnp.dot(a_ref[...], wbf[...], preferred_element_type=F32)
    if has_gate:
        acc = acc * jax.nn.sigmoid(gate_ref[...])
    if has_add:
        acc = acc + add_ref[...]
    o_ref[...] = acc.astype(o_ref.dtype)


def _mm(a, w, layer, *, tn, out_dtype, gate=None, gate_col=0, add=None):
    m, k = a.shape
    n = w.shape[2]
    tm = min(m, 512)
    in_specs = [pl.BlockSpec((tm, k), lambda j, i: (i, 0)),
                pl.BlockSpec((None, k, tn), lambda j, i: (layer, 0, j))]
    args = [a, w]
    if gate is not None:
        goff = gate_col // tn
        in_specs.append(pl.BlockSpec((tm, tn), lambda j, i: (i, goff + j)))
        args.append(gate)
    if add is not None:
        in_specs.append(pl.BlockSpec((tm, tn), lambda j, i: (i, j)))
        args.append(add)
    return pl.pallas_call(
        functools.partial(_mm_kernel, gate is not None, add is not None),
        out_shape=jax.ShapeDtypeStruct((m, n), out_dtype),
        grid=(n // tn, m // tm),
        in_specs=in_specs,
        out_specs=pl.BlockSpec((tm, tn), lambda j, i: (i, j)),
        scratch_shapes=[pltpu.VMEM((k, tn), BF16)],
        compiler_params=_params(2),
        name="matmul",
    )(*args)


def _select_topk(score, lane, k, n_lanes):
    sel = jnp.zeros(score.shape, dtype=jnp.bool_)
    picks = []
    for _ in range(k):
        m = jnp.max(score, axis=-1, keepdims=True)
        cand = jnp.where((score == m) & (m > NEG_INF), lane, n_lanes)
        idx = jnp.min(cand, axis=-1, keepdims=True)
        pick = lane == idx
        sel = sel | pick
        score = jnp.where(pick, NEG_INF, score)
        picks.append(pick)
    return sel, picks


def _select_topk_rows(score, row, k, n_rows):
    sel = jnp.zeros(score.shape, dtype=jnp.bool_)
    for _ in range(k):
        m = jnp.max(score, axis=0, keepdims=True)
        cand = jnp.where((score == m) & (m > NEG_INF), row, n_rows)
        idx = jnp.min(cand, axis=0, keepdims=True)
        pick = row == idx
        sel = sel | pick
        score = jnp.where(pick, NEG_INF, score)
    return sel


def _moba_prompt_kernel(nb, q_ref, k_ref, v_ref, o_ref, kbf, vtb, kmean, sel_s, acc_s):
    i = pl.program_id(2)
    scale = HEAD_DIM ** -0.5

    @pl.when(i == 0)
    def _():
        kf = k_ref[...]
        kbf[...] = kf.reshape(nb // 2, 2 * MOBA_BLOCK, HEAD_DIM).astype(BF16)
        kmean[...] = jnp.mean(kf.reshape(nb, MOBA_BLOCK, HEAD_DIM), axis=1)
        for j in range(nb // 2):
            vtb[j] = v_ref[j * 2 * MOBA_BLOCK:(j + 1) * 2 * MOBA_BLOCK, :].T.astype(BF16)

    q = q_ref[...]
    qt = jnp.concatenate([q[:, :HEAD_DIM].T, q[:, HEAD_DIM:].T], axis=1)
    width = qt.shape[1]
    gate = jnp.dot(kmean[...], qt, precision=lax.Precision.HIGHEST, preferred_element_type=F32)
    blk = lax.broadcasted_iota(jnp.int32, gate.shape, 0)
    gate = jnp.where(blk < i, gate, NEG_INF)
    sel_s[...] = _select_topk_rows(gate, blk, MOBA_TOPK, nb).astype(F32)

    qtb = qt.astype(BF16)

    own_off = pl.multiple_of((i % 2) * MOBA_BLOCK, MOBA_BLOCK)
    k_own = kbf[i // 2, pl.ds(own_off, MOBA_BLOCK), :]
    v_own = v_ref[pl.ds(pl.multiple_of(i * MOBA_BLOCK, MOBA_BLOCK), MOBA_BLOCK), :].T.astype(BF16)
    s = jnp.dot(k_own, qtb, preferred_element_type=F32) * scale
    kpos = lax.broadcasted_iota(jnp.int32, s.shape, 0)
    qpos = lax.broadcasted_iota(jnp.int32, s.shape, 1) % MOBA_BLOCK
    s = jnp.where(kpos <= qpos, s, NEG_INF)
    m0 = jnp.max(s, axis=0, keepdims=True)
    p = jnp.exp(s - m0)
    l0 = jnp.sum(p, axis=0, keepdims=True)
    acc_s[...] = jnp.dot(v_own, p.astype(BF16), preferred_element_type=F32)

    def body(t, carry):
        m, l = carry
        picked = jnp.concatenate(
            [jnp.broadcast_to(sel_s[pl.ds(2 * t, 1), :], (MOBA_BLOCK, width)),
             jnp.broadcast_to(sel_s[pl.ds(2 * t + 1, 1), :], (MOBA_BLOCK, width))], axis=0)
        s = jnp.dot(kbf[t], qtb, preferred_element_type=F32) * scale
        s = jnp.where(picked > 0.0, s, NEG_INF)
        m_new = jnp.maximum(m, jnp.max(s, axis=0, keepdims=True))
        a = jnp.exp(m - m_new)
        p = jnp.exp(s - m_new)
        l = a * l + jnp.sum(p, axis=0, keepdims=True)
        acc_s[...] = a * acc_s[...] + jnp.dot(vtb[t], p.astype(BF16), preferred_element_type=F32)
        return m_new, l

    _, l = lax.fori_loop(0, (i + 1) // 2, body, (m0, l0))
    ot = acc_s[...] / l
    half = width // 2
    o_ref[...] = jnp.concatenate([ot[:, :half].T, ot[:, half:].T], axis=1).astype(o_ref.dtype)


def _moba_prompt(p_all, bsz, seq):
    nb = seq // MOBA_BLOCK
    qw = Q_PER_KV * HEAD_DIM
    return pl.pallas_call(
        functools.partial(_moba_prompt_kernel, nb),
        out_shape=jax.ShapeDtypeStruct((bsz * seq, ATTN_Q_W), BF16),
        grid=(bsz, KV_HEADS, nb),
        in_specs=[pl.BlockSpec((MOBA_BLOCK, qw), lambda b, g, i: (b * nb + i, OFF_QA // qw + g)),
                  pl.BlockSpec((seq, HEAD_DIM), lambda b, g, i: (b, OFF_KA // HEAD_DIM + g)),
                  pl.BlockSpec((seq, HEAD_DIM), lambda b, g, i: (b, OFF_VA // HEAD_DIM + g))],
        out_specs=pl.BlockSpec((MOBA_BLOCK, qw), lambda b, g, i: (b * nb + i, g)),
        scratch_shapes=[pltpu.VMEM((nb // 2, 2 * MOBA_BLOCK, HEAD_DIM), BF16),
                        pltpu.VMEM((nb // 2, HEAD_DIM, 2 * MOBA_BLOCK), BF16),
                        pltpu.VMEM((nb, HEAD_DIM), F32), pltpu.VMEM((nb, Q_PER_KV * MOBA_BLOCK), F32),
                        pltpu.VMEM((HEAD_DIM, Q_PER_KV * MOBA_BLOCK), F32)],
        compiler_params=_params(3),
        name="moba_prompt",
    )(p_all, p_all, p_all)


def _moba_sample_kernel(n_blocks, pages_per_step, pt_ref, q_ref, kn_ref, vn_ref, *rest):
    k_pages = rest[:pages_per_step]
    v_pages = rest[pages_per_step:2 * pages_per_step]
    o_ref, gate_s, m_s, l_s, o_s = rest[2 * pages_per_step:]
    c = pl.program_id(1)
    scale = HEAD_DIM ** -0.5
    page_rows = k_pages[0].shape[0]
    pages_per_block = MOBA_BLOCK // page_rows
    bps = pages_per_step // pages_per_block

    q = q_ref[...]
    qb = q.astype(BF16)
    head = lax.broadcasted_iota(jnp.int32, (N_HEADS, 1), 0)
    lane = lax.broadcasted_iota(jnp.int32, q.shape, 1)

    @pl.when(c == 0)
    def _():
        gate_s[...] = jnp.zeros_like(gate_s)
        m_s[...] = jnp.zeros_like(m_s)
        l_s[...] = jnp.zeros_like(l_s)

    kv_of_col = lax.broadcasted_iota(jnp.int32, (N_HEADS, MOBA_BLOCK * KV_HEADS), 1) % KV_HEADS
    mine = kv_of_col == head // Q_PER_KV
    gcol = lax.broadcasted_iota(jnp.int32, (N_HEADS, KV_HEADS), 1)
    for b in range(bps):
        blk = c * bps + b
        pages = range(b * pages_per_block, (b + 1) * pages_per_block)
        k3 = jnp.concatenate([k_pages[t][...] for t in pages], axis=0)
        v3 = jnp.concatenate([v_pages[t][...] for t in pages], axis=0)
        k_flat = k3.reshape(MOBA_BLOCK * KV_HEADS, HEAD_DIM).astype(BF16)
        v_flat = v3.reshape(MOBA_BLOCK * KV_HEADS, HEAD_DIM).astype(BF16)
        s = lax.dot_general(qb, k_flat, (((1,), (1,)), ((), ())), preferred_element_type=F32) * scale
        s = jnp.where(mine, s, NEG_INF)
        mb = jnp.max(s, axis=-1, keepdims=True)
        pb = jnp.exp(s - mb)
        lb = jnp.sum(pb, axis=-1, keepdims=True)
        ob = jnp.dot(pb.astype(BF16), v_flat, preferred_element_type=F32)
        kmean = jnp.mean(k3, axis=0)
        gate_all = lax.dot_general(q, kmean, (((1,), (1,)), ((), ())),
                                   precision=lax.Precision.HIGHEST, preferred_element_type=F32)
        gt = jnp.sum(jnp.where(gcol == head // Q_PER_KV, gate_all, 0.0), axis=-1, keepdims=True)
        here = lane == blk
        gate_s[...] = jnp.where(here, gt, gate_s[...])
        m_s[...] = jnp.where(here, mb, m_s[...])
        l_s[...] = jnp.where(here, lb, l_s[...])
        o_s[blk] = ob

    @pl.when(c == pl.num_programs(1) - 1)
    def _():
        gate = jnp.where(lane < n_blocks, gate_s[...], NEG_INF)
        sel, _ = _select_topk(gate, lane, MOBA_TOPK, LANES)
        s_self = jnp.sum(q * kn_ref[...], axis=-1, keepdims=True) * scale
        m_all = m_s[...]
        m_fin = jnp.maximum(jnp.max(jnp.where(sel, m_all, NEG_INF), axis=-1, keepdims=True), s_self)
        w = jnp.where(sel, jnp.exp(m_all - m_fin), 0.0)
        w_self = jnp.exp(s_self - m_fin)
        denom = jnp.sum(w * l_s[...], axis=-1, keepdims=True) + w_self
        num = w_self * vn_ref[...]
        for b in range(n_blocks):
            wb = jnp.sum(jnp.where(lane == b, w, 0.0), axis=-1, keepdims=True)
            num = num + wb * o_s[b]
        o_ref[...] = num / denom


def _moba_sample(q, k_new, v_new, cache_k, cache_v, page_table, layer):
    s_cnt = q.shape[0]
    page_rows = cache_k.shape[2]
    n_pages = page_table.shape[0] // s_cnt
    n_blocks = n_pages * page_rows // MOBA_BLOCK
    pages_per_step = min(8, n_pages)
    steps = n_pages // pages_per_step

    def page_spec(t):
        return pl.BlockSpec((None, None, page_rows, KV_HEADS, HEAD_DIM),
                            lambda s, c, pt: (layer, pt[s * n_pages + c * pages_per_step + t], 0, 0, 0))

    head_spec = pl.BlockSpec((None, N_HEADS, HEAD_DIM), lambda s, c, pt: (s, 0, 0))
    grid_spec = pltpu.PrefetchScalarGridSpec(
        num_scalar_prefetch=1,
        grid=(s_cnt, steps),
        in_specs=[head_spec, head_spec, head_spec]
                 + [page_spec(t) for t in range(pages_per_step)] * 2,
        out_specs=head_spec,
        scratch_shapes=[pltpu.VMEM((N_HEADS, LANES), F32), pltpu.VMEM((N_HEADS, LANES), F32),
                        pltpu.VMEM((N_HEADS, LANES), F32), pltpu.VMEM((n_blocks, N_HEADS, HEAD_DIM), F32)],
    )
    return pl.pallas_call(
        functools.partial(_moba_sample_kernel, n_blocks, pages_per_step),
        out_shape=jax.ShapeDtypeStruct((s_cnt, N_HEADS, HEAD_DIM), F32),
        grid_spec=grid_spec,
        compiler_params=_params(2),
        name="moba_sample",
    )(page_table, q, k_new, v_new, *([cache_k] * pages_per_step), *([cache_v] * pages_per_step))


def _group_norm_gate(o, gn, gr):
    mu = jnp.mean(o, axis=-1, keepdims=True)
    do = o - mu
    var = jnp.mean(do * do, axis=-1, keepdims=True)
    return do * lax.rsqrt(var + LN_EPS) * gn * _silu(gr)


def _ret_prompt_kernel(heads, q_ref, k_ref, v_ref, g_ref, cos_ref, sin_ref, dmask_ref, qdec_ref, kdec_ref, sdec_ref,
                       gn_ref, o_ref, s_ref):
    @pl.when(pl.program_id(2) == 0)
    def _():
        s_ref[...] = jnp.zeros_like(s_ref)

    cos = cos_ref[...]
    sin = sin_ref[...]
    half = RET_DK // 2

    def rot(x):
        x1, x2 = x[:, :half], x[:, half:]
        return jnp.concatenate([x1 * cos - x2 * sin, x2 * cos + x1 * sin], axis=1)

    for h in range(heads):
        kc = slice(h * RET_DK, (h + 1) * RET_DK)
        vc = slice(h * RET_DV, (h + 1) * RET_DV)
        q = rot(q_ref[:, kc])
        k = rot(k_ref[:, kc]) * (RET_DK ** -0.5)
        qb = q.astype(BF16)
        vb = v_ref[:, vc].astype(BF16)
        state = s_ref[h]
        scores = lax.dot_general(qb, k.astype(BF16), (((1,), (1,)), ((), ())),
                                 preferred_element_type=F32) * dmask_ref[h]
        o = jnp.dot(scores.astype(BF16), vb, preferred_element_type=F32)
        o = o + jnp.dot(qb, state.astype(BF16), preferred_element_type=F32) * qdec_ref[h]
        kd = (k * kdec_ref[h]).astype(BF16)
        s_ref[h] = state * sdec_ref[h] + lax.dot_general(kd, vb, (((0,), (0,)), ((), ())),
                                                         preferred_element_type=F32)
        o_ref[:, vc] = _group_norm_gate(o, gn_ref[:, vc], g_ref[:, vc]).astype(o_ref.dtype)


def _decay_tables(c):
    lg = jnp.log1p(-jnp.exp2(-5.0 - jnp.arange(RET_HEADS, dtype=F32)))
    i = jnp.arange(c, dtype=F32)
    diff = i[:, None] - i[None, :]
    causal = diff >= 0.0
    dmask = jnp.where(causal[None], jnp.exp(jnp.where(causal, diff, 0.0)[None] * lg[:, None, None]), 0.0)
    qdec = jnp.exp((i[:, None] + 1.0) * lg[None, :]).T[:, :, None]
    kdec = jnp.exp((c - 1.0 - i)[:, None] * lg[None, :]).T[:, :, None]
    sdec = jnp.exp(c * lg)
    return dmask, qdec, kdec, sdec


def _rot_tables(pos):
    half = RET_DK // 2
    theta = ROT_BASE ** -jnp.linspace(0.0, 1.0, half, dtype=F32)
    ang = pos.astype(F32)[:, None] * theta[None, :]
    return jnp.cos(ang), jnp.sin(ang)


def _ret_prompt(p_all, bsz, seq, gn):
    c = min(RET_CHUNK, seq)
    nc = seq // c
    hps = 4
    cos, sin = _rot_tables(jnp.arange(seq, dtype=jnp.int32))
    dmask, qdec, kdec, sdec = _decay_tables(c)
    kw, vw = hps * RET_DK, hps * RET_DV
    head_vec = lambda b, h, j: (h, 0, 0)
    return pl.pallas_call(
        functools.partial(_ret_prompt_kernel, hps),
        out_shape=[jax.ShapeDtypeStruct((bsz * seq, RET_V_W), BF16),
                   jax.ShapeDtypeStruct((bsz, RET_HEADS, RET_DK, RET_DV), F32)],
        grid=(bsz, RET_HEADS // hps, nc),
        in_specs=[pl.BlockSpec((c, kw), lambda b, h, j: (b * nc + j, OFF_QR // kw + h)),
                  pl.BlockSpec((c, kw), lambda b, h, j: (b * nc + j, OFF_KR // kw + h)),
                  pl.BlockSpec((c, vw), lambda b, h, j: (b * nc + j, OFF_VR // vw + h)),
                  pl.BlockSpec((c, vw), lambda b, h, j: (b * nc + j, OFF_GR // vw + h)),
                  pl.BlockSpec((c, RET_DK // 2), lambda b, h, j: (j, 0)),
                  pl.BlockSpec((c, RET_DK // 2), lambda b, h, j: (j, 0)),
                  pl.BlockSpec((hps, c, c), head_vec),
                  pl.BlockSpec((hps, c, 1), head_vec),
                  pl.BlockSpec((hps, c, 1), head_vec),
                  pl.BlockSpec((hps, 1, 1), head_vec),
                  pl.BlockSpec((1, vw), lambda b, h, j: (0, h))],
        out_specs=[pl.BlockSpec((c, vw), lambda b, h, j: (b * nc + j, h)),
                   pl.BlockSpec((None, hps, RET_DK, RET_DV), lambda b, h, j: (b, h, 0, 0))],
        compiler_params=_params(3),
        name="retention_prompt",
    )(p_all, p_all, p_all, p_all, cos, sin, dmask, qdec, kdec, sdec.reshape(RET_HEADS, 1, 1),
      gn.reshape(1, RET_V_W))


def _ret_sample_kernel(gam_ref, q_ref, k_ref, v_ref, g_ref, cos_ref, sin_ref, gn_ref, s_in, *rest):
    o_ref, s_out = rest[-2:]
    cos = cos_ref[...]
    sin = sin_ref[...]
    half = RET_DK // 2

    def rot(x):
        x1, x2 = x[:half], x[half:]
        return jnp.concatenate([x1 * cos - x2 * sin, x2 * cos + x1 * sin], axis=0)

    qc = rot(q_ref[...])
    kc = rot(k_ref[...]) * (RET_DK ** -0.5)
    v = v_ref[...]
    gr = g_ref[...]
    gn = gn_ref[...]
    for h in range(RET_HEADS):
        cols = slice(h * RET_DV, (h + 1) * RET_DV)
        new_state = s_in[h] * gam_ref[h] + kc[:, h:h + 1] * v[:, cols]
        s_out[h] = new_state
        o = jnp.sum(qc[:, h:h + 1] * new_state, axis=0, keepdims=True)
        o_ref[:, cols] = _group_norm_gate(o, gn[:, cols], gr[:, cols])


def _ret_sample(q_t, k_t, v, g, state_ret, layer, pos, gn, prev_out):
    s_cnt = q_t.shape[0]
    cos, sin = _rot_tables(jnp.full((1,), pos, dtype=jnp.int32))
    cos, sin = cos.reshape(-1, 1), sin.reshape(-1, 1)
    gam = _decay_tables(1)[3]
    col_spec = pl.BlockSpec((None, RET_DK, RET_HEADS), lambda s: (s, 0, 0))
    row_spec = pl.BlockSpec((None, 1, RET_V_W), lambda s: (s, 0, 0))
    rot_spec = pl.BlockSpec((RET_DK // 2, 1), lambda s: (0, 0))
    state_spec = pl.BlockSpec((None, None, RET_HEADS, RET_DK, RET_DV), lambda s: (layer, s, 0, 0, 0))
    in_specs = [pl.BlockSpec(memory_space=pltpu.SMEM), col_spec, col_spec, row_spec, row_spec, rot_spec, rot_spec,
                pl.BlockSpec((1, RET_V_W), lambda s: (0, 0)), state_spec]
    args = [gam, q_t, k_t, v, g, cos, sin, gn.reshape(1, RET_V_W), state_ret]
    aliases = {}
    if prev_out is not None:
        in_specs.append(pl.BlockSpec(memory_space=pl.ANY))
        args.append(prev_out)
        aliases = {len(args) - 1: 1}
    return pl.pallas_call(
        _ret_sample_kernel,
        out_shape=[jax.ShapeDtypeStruct((s_cnt, 1, RET_V_W), F32),
                   jax.ShapeDtypeStruct(state_ret.shape, state_ret.dtype)],
        grid=(s_cnt,),
        in_specs=in_specs,
        out_specs=[row_spec, state_spec],
        input_output_aliases=aliases,
        compiler_params=_params(1),
        name="retention_sample",
    )(*args)


def _router_kernel(h_ref, w_ref, b_ref, idx_ref, wt_ref, cnt_ref):
    logits = jnp.dot(h_ref[...], w_ref[...], precision=lax.Precision.HIGHEST, preferred_element_type=F32)
    aff = jax.nn.sigmoid(logits)
    lane = lax.broadcasted_iota(jnp.int32, aff.shape, 1)
    sel, picks = _select_topk(aff + b_ref[...], lane, TOP_K, N_EXPERTS)
    out_lane = lax.broadcasted_iota(jnp.int32, idx_ref.shape, 1)
    idx = jnp.zeros(idx_ref.shape, jnp.int32)
    top_aff = jnp.zeros(idx_ref.shape, F32)
    for t, pick in enumerate(picks):
        e_t = jnp.sum(jnp.where(pick, lane, 0), axis=-1, keepdims=True)
        a_t = jnp.sum(jnp.where(pick, aff, 0.0), axis=-1, keepdims=True)
        idx = jnp.where(out_lane == t, e_t, idx)
        top_aff = jnp.where(out_lane == t, a_t, top_aff)
    idx_ref[...] = idx
    wt_ref[...] = top_aff / jnp.sum(top_aff, axis=-1, keepdims=True) * ROUTED_SCALE
    hist = jnp.sum(sel.astype(F32), axis=0, keepdims=True)
    cnt_ref[...] = jnp.broadcast_to(hist, cnt_ref.shape).astype(jnp.int32)


def _router(h, w_router_l, bias_l):
    m, d = h.shape
    tm = min(m, 512)
    out = pl.BlockSpec((tm, LANES), lambda i: (i, 0))
    idx, wt, cnt = pl.pallas_call(
        _router_kernel,
        out_shape=[jax.ShapeDtypeStruct((m, LANES), jnp.int32), jax.ShapeDtypeStruct((m, LANES), F32),
                   jax.ShapeDtypeStruct((m // tm, SUBLANES, N_EXPERTS), jnp.int32)],
        grid=(m // tm,),
        in_specs=[pl.BlockSpec((tm, d), lambda i: (i, 0)),
                  pl.BlockSpec((d, N_EXPERTS), lambda i: (0, 0)),
                  pl.BlockSpec((1, N_EXPERTS), lambda i: (0, 0))],
        out_specs=[out, out, pl.BlockSpec((None, SUBLANES, N_EXPERTS), lambda i: (i, 0, 0))],
        compiler_params=_params(1),
        name="router",
    )(h, w_router_l, bias_l.reshape(1, N_EXPERTS))
    return idx, wt, jnp.sum(cnt[:, 0, :], axis=0)


def _swiglu_bf16(x, wg, wu, wd):
    g = jnp.dot(x, wg, preferred_element_type=F32)
    u = jnp.dot(x, wu, preferred_element_type=F32)
    return jnp.dot((_silu(g) * u).astype(BF16), wd, preferred_element_type=F32)


def _shared_kernel(h_ref, wg_ref, wu_ref, wd_ref, o_ref, wgb, wub, wdb):
    @pl.when(pl.program_id(0) == 0)
    def _():
        wgb[...] = wg_ref[...].astype(BF16)
        wub[...] = wu_ref[...].astype(BF16)
        wdb[...] = wd_ref[...].astype(BF16)

    o_ref[...] = _swiglu_bf16(h_ref[...].astype(BF16), wgb[...], wub[...], wdb[...])


def _shared_expert(h, wg, wu, wd, layer):
    m, d = h.shape
    tm = min(m, 512)
    return pl.pallas_call(
        _shared_kernel,
        out_shape=jax.ShapeDtypeStruct((m, d), F32),
        grid=(m // tm,),
        in_specs=[pl.BlockSpec((tm, d), lambda i: (i, 0)),
                  pl.BlockSpec((None, d, D_SHARED), lambda i: (layer, 0, 0)),
                  pl.BlockSpec((None, d, D_SHARED), lambda i: (layer, 0, 0)),
                  pl.BlockSpec((None, D_SHARED, d), lambda i: (layer, 0, 0))],
        out_specs=pl.BlockSpec((tm, d), lambda i: (i, 0)),
        scratch_shapes=[pltpu.VMEM((d, D_SHARED), BF16), pltpu.VMEM((d, D_SHARED), BF16),
                        pltpu.VMEM((D_SHARED, d), BF16)],
        compiler_params=_params(1),
        name="shared_expert",
    )(h, wg, wu, wd)


def _experts_kernel(blk_e, src, dst, n_used, h_hbm, wrow_ref, wg_ref, wu_ref, wd_ref, y_hbm,
                    xs, ys, wgb, wub, wdb, gsem, ssem):
    i = pl.program_id(0)
    n_act = n_used[0]
    last = pl.num_programs(0) - 1
    groups = MOE_ROWS // SUBLANES

    def gather(blk, slot, r, src_row):
        return pltpu.make_async_copy(h_hbm.at[src_row], xs.at[slot, r // SUBLANES, :, r % SUBLANES, :],
                                     gsem.at[slot])

    def scatter(blk, slot, r, dst_row):
        return pltpu.make_async_copy(ys.at[slot, r // SUBLANES, :, r % SUBLANES, :], y_hbm.at[dst_row],
                                     ssem.at[slot])

    def start_gathers(blk, slot):
        for r in range(MOE_ROWS):
            gather(blk, slot, r, src[blk * MOE_ROWS + r]).start()

    def wait_gathers(slot):
        for r in range(MOE_ROWS):
            gather(0, slot, r, 0).wait()

    def start_scatters(blk, slot):
        for r in range(MOE_ROWS):
            scatter(blk, slot, r, dst[blk * MOE_ROWS + r]).start()

    def wait_scatters(slot):
        for r in range(MOE_ROWS):
            scatter(0, slot, r, 0).wait()

    @pl.when(i < n_act)
    def _():
        slot = i % 2
        nxt = jnp.minimum(i + 1, last)

        @pl.when(i == 0)
        def _():
            start_gathers(0, 0)

        @pl.when((i == 0) | (blk_e[i] != blk_e[jnp.maximum(i - 1, 0)]))
        def _():
            wgb[...] = wg_ref[...].astype(BF16)
            wub[...] = wu_ref[...].astype(BF16)
            wdb[...] = wd_ref[...].astype(BF16)

        wait_gathers(slot)

        @pl.when(i >= 2)
        def _():
            wait_scatters(slot)

        start_gathers(nxt, 1 - slot)
        x = jnp.concatenate([xs[slot, :, t].reshape(MOE_ROWS, LANES) for t in range(ROW_TILES)], axis=1)
        y = _swiglu_bf16(x.astype(BF16), wgb[...], wub[...], wdb[...]) * wrow_ref[...]
        for t in range(ROW_TILES):
            ys[slot, :, t] = y[:, t * LANES:(t + 1) * LANES].reshape(groups, SUBLANES, LANES)
        start_scatters(i, slot)

        @pl.when(i == n_act - 1)
        def _():
            wait_gathers(1 - slot)
            wait_scatters(slot)

            @pl.when(i >= 1)
            def _():
                wait_scatters(1 - slot)


def _routed_experts(h_rows, top_idx, top_w, counts, wg, wu, wd, layer, n_out_rows):
    m = h_rows.shape[0]
    a = m * TOP_K
    n_blk = -(-(a + N_EXPERTS * (MOE_ROWS - 1)) // MOE_ROWS)
    n_rows = n_blk * MOE_ROWS
    flat_e = top_idx.reshape(a)
    order = jnp.argsort(flat_e).astype(jnp.int32)
    start = jnp.cumsum(counts) - counts
    padded = (counts + MOE_ROWS - 1) // MOE_ROWS * MOE_ROWS
    ends = jnp.cumsum(padded)
    pstart = ends - padded
    blk_e = jnp.minimum(jnp.searchsorted(ends, jnp.arange(n_blk, dtype=jnp.int32) * MOE_ROWS, side='right'),
                        N_EXPERTS - 1).astype(jnp.int32)
    n_used = (ends[-1:] // MOE_ROWS).astype(jnp.int32)
    pos = jnp.arange(n_rows, dtype=jnp.int32)
    e_p = jnp.repeat(blk_e, MOE_ROWS)
    rank = pos - pstart[e_p]
    valid = rank < counts[e_p]
    asg = order[jnp.clip(start[e_p] + rank, 0, a - 1)]
    src = jnp.where(valid, asg // TOP_K, 0)
    dst = jnp.where(valid, asg, a + pos % (2 * MOE_ROWS))
    wrow = jnp.where(valid, top_w.reshape(a)[asg], 0.0).reshape(n_rows, 1)

    d = D_MODEL
    groups = MOE_ROWS // SUBLANES
    wspec = lambda i, be, s, t, n: (layer, be[i], 0, 0)
    grid_spec = pltpu.PrefetchScalarGridSpec(
        num_scalar_prefetch=4,
        grid=(n_blk,),
        in_specs=[pl.BlockSpec(memory_space=pl.ANY),
                  pl.BlockSpec((MOE_ROWS, 1), lambda i, be, s, t, n: (i, 0)),
                  pl.BlockSpec((None, None, d, D_EXPERT), wspec),
                  pl.BlockSpec((None, None, d, D_EXPERT), wspec),
                  pl.BlockSpec((None, None, D_EXPERT, d), wspec)],
        out_specs=pl.BlockSpec(memory_space=pl.ANY),
        scratch_shapes=[pltpu.VMEM((2, groups, ROW_TILES, SUBLANES, LANES), F32),
                        pltpu.VMEM((2, groups, ROW_TILES, SUBLANES, LANES), F32),
                        pltpu.VMEM((d, D_EXPERT), BF16), pltpu.VMEM((d, D_EXPERT), BF16),
                        pltpu.VMEM((D_EXPERT, d), BF16),
                        pltpu.SemaphoreType.DMA((2,)), pltpu.SemaphoreType.DMA((2,))],
    )
    return pl.pallas_call(
        _experts_kernel,
        out_shape=jax.ShapeDtypeStruct((n_out_rows, ROW_TILES, LANES), F32),
        grid_spec=grid_spec,
        compiler_params=_params(1),
        name="routed_experts",
    )(blk_e, src, dst, n_used, h_rows, wrow, wg, wu, wd)


def _combine_kernel(alpha, has_mod, y_ref, sh_ref, x_ref, gt_ref, g_ref, b_ref, *rest):
    if has_mod:
        sc_ref, shift_ref, xo_ref, ho_ref, slab_s, z_s = rest
    else:
        xo_ref, slab_s, z_s = rest
    routed = y_ref[:, 0]
    for k in range(1, TOP_K):
        routed = routed + y_ref[:, k]
    slab_s[...] = routed
    for t in range(ROW_TILES):
        cols = slice(t * LANES, (t + 1) * LANES)
        z_s[:, cols] = sh_ref[:, cols] + slab_s[:, t, :]
    z = alpha * x_ref[...] + gt_ref[...] * z_s[...]
    mu = jnp.mean(z, axis=-1, keepdims=True)
    dz = z - mu
    var = jnp.mean(dz * dz, axis=-1, keepdims=True)
    xn = dz * lax.rsqrt(var + LN_EPS) * g_ref[...] + b_ref[...]
    xo_ref[...] = xn
    if has_mod:
        ho_ref[...] = (xn * (1.0 + sc_ref[...]) + shift_ref[...]).astype(ho_ref.dtype)


def _combine_ln(alpha, y_rows, shared, x, gate, ln_g, ln_b, tok0, sc=None, sh=None):
    g, r, d = x.shape
    tr = min(r, 128)
    per_row = gate.shape[1] == r and r > 1
    has_mod = sc is not None
    nt = r // tr
    t0 = tok0 // tr
    tile = pl.BlockSpec((None, tr, d), lambda g, i: (g, i, 0))
    flat = pl.BlockSpec((tr, d), lambda g, i: (g * nt + i, 0))
    vec = pl.BlockSpec((1, d), lambda g, i: (0, 0))
    in_specs = [pl.BlockSpec((tr, TOP_K, ROW_TILES, LANES), lambda g, i: (t0 + g * nt + i, 0, 0, 0)),
                flat, tile, _row_spec(tr, d, per_row), vec, vec]
    args = [y_rows, shared, x, gate, ln_g.reshape(1, d), ln_b.reshape(1, d)]
    out_shape = [jax.ShapeDtypeStruct((g, r, d), F32)]
    out_specs = [tile]
    if has_mod:
        in_specs += [_row_spec(tr, d, per_row), _row_spec(tr, d, per_row)]
        args += [sc, sh]
        out_shape.append(jax.ShapeDtypeStruct((g, r, d), BF16))
        out_specs.append(tile)
    res = pl.pallas_call(
        functools.partial(_combine_kernel, alpha, has_mod),
        out_shape=out_shape,
        grid=(g, nt),
        in_specs=in_specs,
        out_specs=out_specs,
        scratch_shapes=[pltpu.VMEM((tr, ROW_TILES, LANES), F32), pltpu.VMEM((tr, d), F32)],
        compiler_params=_params(2),
        name="moe_combine_ln",
    )(*args)
    return res if has_mod else (res[0], None)


def kernel(x_prompt, x_sample, cache_k, cache_v, state_ret, page_table, c_prompt, c_sample, w_ada, b_ada, w_in, w_attn_br, ret_norm_g, w_ret_br, w_o, ln_g, ln_b, w_router, router_bias, w_exp_gate, w_exp_up, w_exp_down, w_sh_gate, w_sh_up, w_sh_down):
    bsz, seq, d = x_prompt.shape
    s_cnt = x_sample.shape[0]
    depth = w_in.shape[0]
    alpha = (2.0 * depth) ** 0.25
    page_rows = cache_k.shape[2]
    past = page_table.shape[1] * page_rows
    m_p = bsz * seq
    m_all = m_p + s_cnt
    y_tokens = m_all + -(-2 * MOE_ROWS // TOP_K)

    n_c = bsz + s_cnt
    r_c = -(-n_c // SUBLANES) * SUBLANES
    c_all = jnp.concatenate([c_prompt, c_sample, jnp.zeros((r_c - n_c, d), F32)], axis=0)
    mod = _adaln(c_all, w_ada, b_ada)
    mod_p = mod[:, :bsz].reshape(depth, bsz, N_MOD, 1, d)
    mod_s = mod[:, bsz:n_c].reshape(depth, 1, s_cnt, N_MOD, d)

    def mods(layer):
        mp = [mod_p[layer, :, j] for j in range(N_MOD)]
        ms = [mod_s[layer, :, :, j] for j in range(N_MOD)]
        return mp, ms

    pt_flat = page_table.reshape(-1)

    xp = x_prompt
    xs = x_sample.reshape(1, s_cnt, d)
    mp, ms = mods(0)
    hp = _modulate(xp, mp[1], mp[0], BF16)
    hs = _modulate(xs, ms[1], ms[0], BF16)

    kp, vp, sp, kd, vd = [], [], [], [], []
    state_out = None
    for layer in range(depth):
        mp, ms = mods(layer)
        pp = _mm(hp.reshape(m_p, d), w_in, layer, tn=1024, out_dtype=F32)
        o_a = _moba_prompt(pp, bsz, seq)
        o_r, s_new = _ret_prompt(pp, bsz, seq, ret_norm_g[layer])
        t1 = _mm(o_a, w_attn_br, layer, tn=512, out_dtype=F32, gate=pp, gate_col=OFF_BA)
        mixed = _mm(o_r, w_ret_br, layer, tn=512, out_dtype=BF16, gate=pp, gate_col=OFF_BR, add=t1)
        y_p = _mm(mixed, w_o, layer, tn=512, out_dtype=F32).reshape(bsz, seq, d)
        xp, h2p, hr_p = _post_ln(alpha, xp, y_p, mp[2], ln_g[layer, 0], ln_b[layer, 0], mp[4], mp[3])
        kp.append(pp[:, OFF_KA:OFF_VA].reshape(bsz, seq, KV_HEADS, HEAD_DIM))
        vp.append(pp[:, OFF_VA:OFF_QR].reshape(bsz, seq, KV_HEADS, HEAD_DIM))
        sp.append(s_new)

        ps = _mm(hs.reshape(s_cnt, d), w_in, layer, tn=1024, out_dtype=F32)
        k_new = ps[:, OFF_KA:OFF_VA].reshape(s_cnt, KV_HEADS, HEAD_DIM)
        v_new = ps[:, OFF_VA:OFF_QR].reshape(s_cnt, KV_HEADS, HEAD_DIM)
        o_as = _moba_sample(ps[:, OFF_QA:OFF_KA].reshape(s_cnt, N_HEADS, HEAD_DIM),
                            jnp.repeat(k_new, Q_PER_KV, axis=1), jnp.repeat(v_new, Q_PER_KV, axis=1),
                            cache_k, cache_v, pt_flat, layer)
        q_t = ps[:, OFF_QR:OFF_KR].reshape(s_cnt, RET_HEADS, RET_DK).transpose(0, 2, 1)
        k_t = ps[:, OFF_KR:OFF_VR].reshape(s_cnt, RET_HEADS, RET_DK).transpose(0, 2, 1)
        o_rs, state_out = _ret_sample(q_t, k_t, ps[:, OFF_VR:OFF_GR].reshape(s_cnt, 1, RET_V_W),
                                      ps[:, OFF_GR:OFF_BA].reshape(s_cnt, 1, RET_V_W),
                                      state_ret, layer, past, ret_norm_g[layer], state_out)
        t1s = _mm(o_as.reshape(s_cnt, ATTN_Q_W).astype(BF16), w_attn_br, layer, tn=512, out_dtype=F32,
                  gate=ps, gate_col=OFF_BA)
        mixed_s = _mm(o_rs.reshape(s_cnt, RET_V_W).astype(BF16), w_ret_br, layer, tn=512, out_dtype=BF16,
                      gate=ps, gate_col=OFF_BR, add=t1s)
        y_s = _mm(mixed_s, w_o, layer, tn=512, out_dtype=F32).reshape(1, s_cnt, d)
        xs, h2s, hr_s = _post_ln(alpha, xs, y_s, ms[2], ln_g[layer, 0], ln_b[layer, 0], ms[4], ms[3])
        kd.append(k_new.reshape(s_cnt, 1, KV_HEADS, HEAD_DIM))
        vd.append(v_new.reshape(s_cnt, 1, KV_HEADS, HEAD_DIM))

        h2p = h2p.reshape(m_p, d)
        h2s = h2s.reshape(s_cnt, d)
        idx_p, wt_p, cnt_p = _router(h2p, w_router[layer], router_bias[layer])
        idx_s, wt_s, cnt_s = _router(h2s, w_router[layer], router_bias[layer])
        sh_p = _shared_expert(h2p, w_sh_gate, w_sh_up, w_sh_down, layer)
        sh_s = _shared_expert(h2s, w_sh_gate, w_sh_up, w_sh_down, layer)
        h_rows = jnp.concatenate([hr_p, hr_s], axis=0)
        top_idx = jnp.concatenate([idx_p[:, :TOP_K], idx_s[:, :TOP_K]], axis=0)
        top_w = jnp.concatenate([wt_p[:, :TOP_K], wt_s[:, :TOP_K]], axis=0)
        y_rows = _routed_experts(h_rows, top_idx, top_w, cnt_p + cnt_s, w_exp_gate, w_exp_up, w_exp_down, layer,
                                 y_tokens * TOP_K)
        y_rows = y_rows.reshape(y_tokens, TOP_K, ROW_TILES, LANES)
        if layer + 1 < depth:
            mp_n, ms_n = mods(layer + 1)
            nxt_p, nxt_s = (mp_n[1], mp_n[0]), (ms_n[1], ms_n[0])
        else:
            nxt_p = nxt_s = (None, None)
        xp, hp = _combine_ln(alpha, y_rows, sh_p, xp, mp[5], ln_g[layer, 1], ln_b[layer, 1], 0, *nxt_p)
        xs, hs = _combine_ln(alpha, y_rows, sh_s, xs, ms[5], ln_g[layer, 1], ln_b[layer, 1], m_p, *nxt_s)

    return (xp, xs.reshape(s_cnt, 1, d), jnp.stack(kp), jnp.stack(vp), jnp.stack(sp).astype(state_ret.dtype),
            jnp.stack(kd), jnp.stack(vd), state_out)
```

```python
import functools

import jax
import jax.numpy as jnp
import numpy as np
from jax import lax
from jax.experimental import pallas as pl
from jax.experimental.pallas import tpu as pltpu

F32 = jnp.float32
BF16 = jnp.bfloat16

D_MODEL = 2048
N_HEADS = 16
KV_HEADS = 8
HEAD_DIM = 128
Q_PER_KV = N_HEADS // KV_HEADS
MOBA_BLOCK = 256
MOBA_TOPK = 3
RET_HEADS = 8
RET_DK = D_MODEL // RET_HEADS
RET_DV = 2 * D_MODEL // RET_HEADS
RET_CHUNK = 128
ROT_BASE = 10000.0
N_EXPERTS = 64
TOP_K = 6
D_EXPERT = D_MODEL // 4
D_SHARED = D_MODEL // 4
ROUTED_SCALE = 2.5
LN_EPS = 1e-5
N_MOD = 6

ATTN_Q_W = N_HEADS * HEAD_DIM
ATTN_KV_W = KV_HEADS * HEAD_DIM
RET_QK_W = RET_HEADS * RET_DK
RET_V_W = RET_HEADS * RET_DV
OFF_QA = 0
OFF_KA = OFF_QA + ATTN_Q_W
OFF_VA = OFF_KA + ATTN_KV_W
OFF_QR = OFF_VA + ATTN_KV_W
OFF_KR = OFF_QR + RET_QK_W
OFF_VR = OFF_KR + RET_QK_W
OFF_GR = OFF_VR + RET_V_W
OFF_BA = OFF_GR + RET_V_W
OFF_BR = OFF_BA + D_MODEL
D_IN = OFF_BR + D_MODEL

LANES = 128
SUBLANES = 8
VMEM_LIMIT_BYTES = 52 * 1024 * 1024
MOE_ROWS = 256
ROW_TILES = D_MODEL // LANES

NEG_INF = float("-inf")


def _params(n_axes):
    return pltpu.CompilerParams(dimension_semantics=("arbitrary",) * n_axes,
                                vmem_limit_bytes=VMEM_LIMIT_BYTES)


def _silu(x):
    return x * jax.nn.sigmoid(x)


def _ada_kernel(c_ref, w_ref, b_ref, o_ref):
    a = _silu(c_ref[...]).astype(BF16)
    o_ref[...] = jnp.dot(a, w_ref[...].astype(BF16), preferred_element_type=F32) + b_ref[...]


def _adaln(c_all, w_ada, b_ada):
    depth, d, n = w_ada.shape
    r = c_all.shape[0]
    tn = 1024
    return pl.pallas_call(
        _ada_kernel,
        out_shape=jax.ShapeDtypeStruct((depth, r, n), F32),
        grid=(depth, n // tn),
        in_specs=[pl.BlockSpec((r, d), lambda l, j: (0, 0)),
                  pl.BlockSpec((None, d, tn), lambda l, j: (l, 0, j)),
                  pl.BlockSpec((None, 1, tn), lambda l, j: (l, 0, j))],
        out_specs=pl.BlockSpec((None, r, tn), lambda l, j: (l, 0, j)),
        compiler_params=_params(2),
        name="adaln",
    )(c_all, w_ada, b_ada.reshape(depth, 1, n))


def _row_spec(tr, d, per_row):
    if per_row:
        return pl.BlockSpec((None, tr, d), lambda g, i: (g, i, 0))
    return pl.BlockSpec((None, 1, d), lambda g, i: (g, 0, 0))


def _modulate_kernel(x_ref, sc_ref, sh_ref, o_ref):
    o_ref[...] = (x_ref[...] * (1.0 + sc_ref[...]) + sh_ref[...]).astype(o_ref.dtype)


def _modulate(x, sc, sh, out_dtype):
    g, r, d = x.shape
    tr = min(r, 512)
    per_row = sc.shape[1] == r and r > 1
    return pl.pallas_call(
        _modulate_kernel,
        out_shape=jax.ShapeDtypeStruct((g, r, d), out_dtype),
        grid=(g, r // tr),
        in_specs=[pl.BlockSpec((None, tr, d), lambda g, i: (g, i, 0)),
                  _row_spec(tr, d, per_row), _row_spec(tr, d, per_row)],
        out_specs=pl.BlockSpec((None, tr, d), lambda g, i: (g, i, 0)),
        compiler_params=_params(2),
        name="modulate",
    )(x, sc, sh)


def _ln_kernel(alpha, has_mod, x_ref, y_ref, gt_ref, g_ref, b_ref, *rest):
    if has_mod:
        sc_ref, sh_ref, xo_ref, ho_ref, hr_ref = rest
    else:
        (xo_ref,) = rest
    z = alpha * x_ref[...] + gt_ref[...] * y_ref[...]
    mu = jnp.mean(z, axis=-1, keepdims=True)
    dz = z - mu
    var = jnp.mean(dz * dz, axis=-1, keepdims=True)
    xn = dz * lax.rsqrt(var + LN_EPS) * g_ref[...] + b_ref[...]
    xo_ref[...] = xn
    if has_mod:
        h = xn * (1.0 + sc_ref[...]) + sh_ref[...]
        ho_ref[...] = h
        for t in range(ROW_TILES):
            hr_ref[:, t, :] = h[:, t * LANES:(t + 1) * LANES]


def _post_ln(alpha, x, y, gate, ln_g, ln_b, sc=None, sh=None):
    g, r, d = x.shape
    tr = min(r, 256)
    per_row = gate.shape[1] == r and r > 1
    has_mod = sc is not None
    tile = pl.BlockSpec((None, tr, d), lambda g, i: (g, i, 0))
    vec = pl.BlockSpec((1, d), lambda g, i: (0, 0))
    in_specs = [tile, tile, _row_spec(tr, d, per_row), vec, vec]
    args = [x, y, gate, ln_g.reshape(1, d), ln_b.reshape(1, d)]
    out_shape = [jax.ShapeDtypeStruct((g, r, d), F32)]
    out_specs = [tile]
    if has_mod:
        in_specs += [_row_spec(tr, d, per_row), _row_spec(tr, d, per_row)]
        args += [sc, sh]
        nt = r // tr
        out_shape += [jax.ShapeDtypeStruct((g, r, d), F32), jax.ShapeDtypeStruct((g * r, ROW_TILES, LANES), F32)]
        out_specs += [tile, pl.BlockSpec((tr, ROW_TILES, LANES), lambda g, i: (g * nt + i, 0, 0))]
    res = pl.pallas_call(
        functools.partial(_ln_kernel, alpha, has_mod),
        out_shape=out_shape,
        grid=(g, r // tr),
        in_specs=in_specs,
        out_specs=out_specs,
        compiler_params=_params(2),
        name="post_ln",
    )(*args)
    return res if has_mod else (res[0], None, None)


def _mm_kernel(has_gate, has_add, a_ref, w_ref, *rest):
    rest = list(rest)
    gate_ref = rest.pop(0) if has_gate else None
    add_ref = rest.pop(0) if has_add else None
    o_ref, wbf = rest

    @pl.when(pl.program_id(1) == 0)
    def _():
        wbf[...] = w_ref[...].astype(BF16)

    acc = jnp.dot(a_ref[...], wbf[...], preferred_element_type=F32)
    if has_gate:
        acc = acc * jax.nn.sigmoid(gate_ref[...])
    if has_add:
        acc = acc + add_ref[...]
    o_ref[...] = acc.astype(o_ref.dtype)


def _mm(a, w, layer, *, tn, out_dtype, gate=None, gate_col=0, add=None):
    m, k = a.shape
    n = w.shape[2]
    tm = min(m, 512)
    in_specs = [pl.BlockSpec((tm, k), lambda j, i: (i, 0)),
                pl.BlockSpec((None, k, tn), lambda j, i: (layer, 0, j))]
    args = [a, w]
    if gate is not None:
        goff = gate_col // tn
        in_specs.append(pl.BlockSpec((tm, tn), lambda j, i: (i, goff + j)))
        args.append(gate)
    if add is not None:
        in_specs.append(pl.BlockSpec((tm, tn), lambda j, i: (i, j)))
        args.append(add)
    return pl.pallas_call(
        functools.partial(_mm_kernel, gate is not None, add is not None),
        out_shape=jax.ShapeDtypeStruct((m, n), out_dtype),
        grid=(n // tn, m // tm),
        in_specs=in_specs,
        out_specs=pl.BlockSpec((tm, tn), lambda j, i: (i, j)),
        scratch_shapes=[pltpu.VMEM((k, tn), BF16)],
        compiler_params=_params(2),
        name="matmul",
    )(*args)


def _select_topk(score, lane, k, n_lanes):
    sel = jnp.zeros(score.shape, dtype=jnp.bool_)
    picks = []
    for _ in range(k):
        m = jnp.max(score, axis=-1, keepdims=True)
        cand = jnp.where((score == m) & (m > NEG_INF), lane, n_lanes)
        idx = jnp.min(cand, axis=-1, keepdims=True)
        pick = lane == idx
        sel = sel | pick
        score = jnp.where(pick, NEG_INF, score)
        picks.append(pick)
    return sel, picks


def _select_topk_rows(score, row, k, n_rows):
    sel = jnp.zeros(score.shape, dtype=jnp.bool_)
    for _ in range(k):
        m = jnp.max(score, axis=0, keepdims=True)
        cand = jnp.where((score == m) & (m > NEG_INF), row, n_rows)
        idx = jnp.min(cand, axis=0, keepdims=True)
        pick = row == idx
        sel = sel | pick
        score = jnp.where(pick, NEG_INF, score)
    return sel


def _moba_prompt_kernel(nb, q_ref, k_ref, v_ref, o_ref, kbf, vtb, kmean, sel_s, acc_s):
    i = pl.program_id(2)
    scale = HEAD_DIM ** -0.5

    @pl.when(i == 0)
    def _():
        kf = k_ref[...]
        kbf[...] = kf.reshape(nb // 2, 2 * MOBA_BLOCK, HEAD_DIM).astype(BF16)
        kmean[...] = jnp.mean(kf.reshape(nb, MOBA_BLOCK, HEAD_DIM), axis=1)
        for j in range(nb // 2):
            vtb[j] = v_ref[j * 2 * MOBA_BLOCK:(j + 1) * 2 * MOBA_BLOCK, :].T.astype(BF16)

    q = q_ref[...]
    qt = jnp.concatenate([q[:, :HEAD_DIM].T, q[:, HEAD_DIM:].T], axis=1)
    width = qt.shape[1]
    gate = jnp.dot(kmean[...], qt, precision=lax.Precision.HIGHEST, preferred_element_type=F32)
    blk = lax.broadcasted_iota(jnp.int32, gate.shape, 0)
    gate = jnp.where(blk < i, gate, NEG_INF)
    sel_s[...] = _select_topk_rows(gate, blk, MOBA_TOPK, nb).astype(F32)

    qtb = qt.astype(BF16)

    own_off = pl.multiple_of((i % 2) * MOBA_BLOCK, MOBA_BLOCK)
    k_own = kbf[i // 2, pl.ds(own_off, MOBA_BLOCK), :]
    v_own = v_ref[pl.ds(pl.multiple_of(i * MOBA_BLOCK, MOBA_BLOCK), MOBA_BLOCK), :].T.astype(BF16)
    s = jnp.dot(k_own, qtb, preferred_element_type=F32) * scale
    kpos = lax.broadcasted_iota(jnp.int32, s.shape, 0)
    qpos = lax.broadcasted_iota(jnp.int32, s.shape, 1) % MOBA_BLOCK
    s = jnp.where(kpos <= qpos, s, NEG_INF)
    m0 = jnp.max(s, axis=0, keepdims=True)
    p = jnp.exp(s - m0)
    l0 = jnp.sum(p, axis=0, keepdims=True)
    acc_s[...] = jnp.dot(v_own, p.astype(BF16), preferred_element_type=F32)

    def body(t, carry):
        m, l = carry
        picked = jnp.concatenate(
            [jnp.broadcast_to(sel_s[pl.ds(2 * t, 1), :], (MOBA_BLOCK, width)),
             jnp.broadcast_to(sel_s[pl.ds(2 * t + 1, 1), :], (MOBA_BLOCK, width))], axis=0)
        s = jnp.dot(kbf[t], qtb, preferred_element_type=F32) * scale
        s = jnp.where(picked > 0.0, s, NEG_INF)
        m_new = jnp.maximum(m, jnp.max(s, axis=0, keepdims=True))
        a = jnp.exp(m - m_new)
        p = jnp.exp(s - m_new)
        l = a * l + jnp.sum(p, axis=0, keepdims=True)
        acc_s[...] = a * acc_s[...] + jnp.dot(vtb[t], p.astype(BF16), preferred_element_type=F32)
        return m_new, l

    _, l = lax.fori_loop(0, (i + 1) // 2, body, (m0, l0))
    ot = acc_s[...] / l
    half = width // 2
    o_ref[...] = jnp.concatenate([ot[:, :half].T, ot[:, half:].T], axis=1).astype(o_ref.dtype)


def _moba_prompt(p_all, bsz, seq):
    nb = seq // MOBA_BLOCK
    qw = Q_PER_KV * HEAD_DIM
    return pl.pallas_call(
        functools.partial(_moba_prompt_kernel, nb),
        out_shape=jax.ShapeDtypeStruct((bsz * seq, ATTN_Q_W), BF16),
        grid=(bsz, KV_HEADS, nb),
        in_specs=[pl.BlockSpec((MOBA_BLOCK, qw), lambda b, g, i: (b * nb + i, OFF_QA // qw + g)),
                  pl.BlockSpec((seq, HEAD_DIM), lambda b, g, i: (b, OFF_KA // HEAD_DIM + g)),
                  pl.BlockSpec((seq, HEAD_DIM), lambda b, g, i: (b, OFF_VA // HEAD_DIM + g))],
        out_specs=pl.BlockSpec((MOBA_BLOCK, qw), lambda b, g, i: (b * nb + i, g)),
        scratch_shapes=[pltpu.VMEM((nb // 2, 2 * MOBA_BLOCK, HEAD_DIM), BF16),
                        pltpu.VMEM((nb // 2, HEAD_DIM, 2 * MOBA_BLOCK), BF16),
                        pltpu.VMEM((nb, HEAD_DIM), F32), pltpu.VMEM((nb, Q_PER_KV * MOBA_BLOCK), F32),
                        pltpu.VMEM((HEAD_DIM, Q_PER_KV * MOBA_BLOCK), F32)],
        compiler_params=_params(3),
        name="moba_prompt",
    )(p_all, p_all, p_all)


def _moba_sample_kernel(n_blocks, pages_per_step, pt_ref, q_ref, kn_ref, vn_ref, *rest):
    k_pages = rest[:pages_per_step]
    v_pages = rest[pages_per_step:2 * pages_per_step]
    o_ref, gate_s, m_s, l_s, o_s = rest[2 * pages_per_step:]
    c = pl.program_id(1)
    scale = HEAD_DIM ** -0.5
    page_rows = k_pages[0].shape[0]
    pages_per_block = MOBA_BLOCK // page_rows
    bps = pages_per_step // pages_per_block

    q = q_ref[...]
    qb = q.astype(BF16)
    head = lax.broadcasted_iota(jnp.int32, (N_HEADS, 1), 0)
    lane = lax.broadcasted_iota(jnp.int32, q.shape, 1)

    @pl.when(c == 0)
    def _():
        gate_s[...] = jnp.zeros_like(gate_s)
        m_s[...] = jnp.zeros_like(m_s)
        l_s[...] = jnp.zeros_like(l_s)

    kv_of_col = lax.broadcasted_iota(jnp.int32, (N_HEADS, MOBA_BLOCK * KV_HEADS), 1) % KV_HEADS
    mine = kv_of_col == head // Q_PER_KV
    gcol = lax.broadcasted_iota(jnp.int32, (N_HEADS, KV_HEADS), 1)
    for b in range(bps):
        blk = c * bps + b
        pages = range(b * pages_per_block, (b + 1) * pages_per_block)
        k3 = jnp.concatenate([k_pages[t][...] for t in pages], axis=0)
        v3 = jnp.concatenate([v_pages[t][...] for t in pages], axis=0)
        k_flat = k3.reshape(MOBA_BLOCK * KV_HEADS, HEAD_DIM).astype(BF16)
        v_flat = v3.reshape(MOBA_BLOCK * KV_HEADS, HEAD_DIM).astype(BF16)
        s = lax.dot_general(qb, k_flat, (((1,), (1,)), ((), ())), preferred_element_type=F32) * scale
        s = jnp.where(mine, s, NEG_INF)
        mb = jnp.max(s, axis=-1, keepdims=True)
        pb = jnp.exp(s - mb)
        lb = jnp.sum(pb, axis=-1, keepdims=True)
        ob = jnp.dot(pb.astype(BF16), v_flat, preferred_element_type=F32)
        kmean = jnp.mean(k3, axis=0)
        gate_all = lax.dot_general(q, kmean, (((1,), (1,)), ((), ())),
                                   precision=lax.Precision.HIGHEST, preferred_element_type=F32)
        gt = jnp.sum(jnp.where(gcol == head // Q_PER_KV, gate_all, 0.0), axis=-1, keepdims=True)
        here = lane == blk
        gate_s[...] = jnp.where(here, gt, gate_s[...])
        m_s[...] = jnp.where(here, mb, m_s[...])
        l_s[...] = jnp.where(here, lb, l_s[...])
        o_s[blk] = ob

    @pl.when(c == pl.num_programs(1) - 1)
    def _():
        gate = jnp.where(lane < n_blocks, gate_s[...], NEG_INF)
        sel, _ = _select_topk(gate, lane, MOBA_TOPK, LANES)
        s_self = jnp.sum(q * kn_ref[...], axis=-1, keepdims=True) * scale
        m_all = m_s[...]
        m_fin = jnp.maximum(jnp.max(jnp.where(sel, m_all, NEG_INF), axis=-1, keepdims=True), s_self)
        w = jnp.where(sel, jnp.exp(m_all - m_fin), 0.0)
        w_self = jnp.exp(s_self - m_fin)
        denom = jnp.sum(w * l_s[...], axis=-1, keepdims=True) + w_self
        num = w_self * vn_ref[...]
        for b in range(n_blocks):
            wb = jnp.sum(jnp.where(lane == b, w, 0.0), axis=-1, keepdims=True)
            num = num + wb * o_s[b]
        o_ref[...] = num / denom


def _moba_sample(q, k_new, v_new, cache_k, cache_v, page_table, layer):
    s_cnt = q.shape[0]
    page_rows = cache_k.shape[2]
    n_pages = page_table.shape[0] // s_cnt
    n_blocks = n_pages * page_rows // MOBA_BLOCK
    pages_per_step = min(8, n_pages)
    steps = n_pages // pages_per_step

    def page_spec(t):
        return pl.BlockSpec((None, None, page_rows, KV_HEADS, HEAD_DIM),
                            lambda s, c, pt: (layer, pt[s * n_pages + c * pages_per_step + t], 0, 0, 0))

    head_spec = pl.BlockSpec((None, N_HEADS, HEAD_DIM), lambda s, c, pt: (s, 0, 0))
    grid_spec = pltpu.PrefetchScalarGridSpec(
        num_scalar_prefetch=1,
        grid=(s_cnt, steps),
        in_specs=[head_spec, head_spec, head_spec]
                 + [page_spec(t) for t in range(pages_per_step)] * 2,
        out_specs=head_spec,
        scratch_shapes=[pltpu.VMEM((N_HEADS, LANES), F32), pltpu.VMEM((N_HEADS, LANES), F32),
                        pltpu.VMEM((N_HEADS, LANES), F32), pltpu.VMEM((n_blocks, N_HEADS, HEAD_DIM), F32)],
    )
    return pl.pallas_call(
        functools.partial(_moba_sample_kernel, n_blocks, pages_per_step),
        out_shape=jax.ShapeDtypeStruct((s_cnt, N_HEADS, HEAD_DIM), F32),
        grid_spec=grid_spec,
        compiler_params=_params(2),
        name="moba_sample",
    )(page_table, q, k_new, v_new, *([cache_k] * pages_per_step), *([cache_v] * pages_per_step))


def _group_norm_gate(o, gn, gr):
    mu = jnp.mean(o, axis=-1, keepdims=True)
    do = o - mu
    var = jnp.mean(do * do, axis=-1, keepdims=True)
    return do * lax.rsqrt(var + LN_EPS) * gn * _silu(gr)


def _ret_prompt_kernel(heads, q_ref, k_ref, v_ref, g_ref, cos_ref, sin_ref, dmask_ref, qdec_ref, kdec_ref, sdec_ref,
                       gn_ref, o_ref, s_ref):
    @pl.when(pl.program_id(2) == 0)
    def _():
        s_ref[...] = jnp.zeros_like(s_ref)

    cos = cos_ref[...]
    sin = sin_ref[...]
    half = RET_DK // 2

    def rot(x):
        x1, x2 = x[:, :half], x[:, half:]
        return jnp.concatenate([x1 * cos - x2 * sin, x2 * cos + x1 * sin], axis=1)

    for h in range(heads):
        kc = slice(h * RET_DK, (h + 1) * RET_DK)
        vc = slice(h * RET_DV, (h + 1) * RET_DV)
        q = rot(q_ref[:, kc])
        k = rot(k_ref[:, kc]) * (RET_DK ** -0.5)
        qb = q.astype(BF16)
        vb = v_ref[:, vc].astype(BF16)
        state = s_ref[h]
        scores = lax.dot_general(qb, k.astype(BF16), (((1,), (1,)), ((), ())),
                                 preferred_element_type=F32) * dmask_ref[h]
        o = jnp.dot(scores.astype(BF16), vb, preferred_element_type=F32)
        o = o + jnp.dot(qb, state.astype(BF16), preferred_element_type=F32) * qdec_ref[h]
        kd = (k * kdec_ref[h]).astype(BF16)
        s_ref[h] = state * sdec_ref[h] + lax.dot_general(kd, vb, (((0,), (0,)), ((), ())),
                                                         preferred_element_type=F32)
        o_ref[:, vc] = _group_norm_gate(o, gn_ref[:, vc], g_ref[:, vc]).astype(o_ref.dtype)


def _decay_tables(c):
    lg = jnp.log1p(-jnp.exp2(-5.0 - jnp.arange(RET_HEADS, dtype=F32)))
    i = jnp.arange(c, dtype=F32)
    diff = i[:, None] - i[None, :]
    causal = diff >= 0.0
    dmask = jnp.where(causal[None], jnp.exp(jnp.where(causal, diff, 0.0)[None] * lg[:, None, None]), 0.0)
    qdec = jnp.exp((i[:, None] + 1.0) * lg[None, :]).T[:, :, None]
    kdec = jnp.exp((c - 1.0 - i)[:, None] * lg[None, :]).T[:, :, None]
    sdec = jnp.exp(c * lg)
    return dmask, qdec, kdec, sdec


def _rot_tables(pos):
    half = RET_DK // 2
    theta = ROT_BASE ** -jnp.linspace(0.0, 1.0, half, dtype=F32)
    ang = pos.astype(F32)[:, None] * theta[None, :]
    return jnp.cos(ang), jnp.sin(ang)


def _ret_prompt(p_all, bsz, seq, gn):
    c = min(RET_CHUNK, seq)
    nc = seq // c
    hps = 4
    cos, sin = _rot_tables(jnp.arange(seq, dtype=jnp.int32))
    dmask, qdec, kdec, sdec = _decay_tables(c)
    kw, vw = hps * RET_DK, hps * RET_DV
    head_vec = lambda b, h, j: (h, 0, 0)
    return pl.pallas_call(
        functools.partial(_ret_prompt_kernel, hps),
        out_shape=[jax.ShapeDtypeStruct((bsz * seq, RET_V_W), BF16),
                   jax.ShapeDtypeStruct((bsz, RET_HEADS, RET_DK, RET_DV), F32)],
        grid=(bsz, RET_HEADS // hps, nc),
        in_specs=[pl.BlockSpec((c, kw), lambda b, h, j: (b * nc + j, OFF_QR // kw + h)),
                  pl.BlockSpec((c, kw), lambda b, h, j: (b * nc + j, OFF_KR // kw + h)),
                  pl.BlockSpec((c, vw), lambda b, h, j: (b * nc + j, OFF_VR // vw + h)),
                  pl.BlockSpec((c, vw), lambda b, h, j: (b * nc + j, OFF_GR // vw + h)),
                  pl.BlockSpec((c, RET_DK // 2), lambda b, h, j: (j, 0)),
                  pl.BlockSpec((c, RET_DK // 2), lambda b, h, j: (j, 0)),
                  pl.BlockSpec((hps, c, c), head_vec),
                  pl.BlockSpec((hps, c, 1), head_vec),
                  pl.BlockSpec((hps, c, 1), head_vec),
                  pl.BlockSpec((hps, 1, 1), head_vec),
                  pl.BlockSpec((1, vw), lambda b, h, j: (0, h))],
        out_specs=[pl.BlockSpec((c, vw), lambda b, h, j: (b * nc + j, h)),
                   pl.BlockSpec((None, hps, RET_DK, RET_DV), lambda b, h, j: (b, h, 0, 0))],
        compiler_params=_params(3),
        name="retention_prompt",
    )(p_all, p_all, p_all, p_all, cos, sin, dmask, qdec, kdec, sdec.reshape(RET_HEADS, 1, 1),
      gn.reshape(1, RET_V_W))


def _ret_sample_kernel(gam_ref, q_ref, k_ref, v_ref, g_ref, cos_ref, sin_ref, gn_ref, s_in, *rest):
    o_ref, s_out = rest[-2:]
    cos = cos_ref[...]
    sin = sin_ref[...]
    half = RET_DK // 2

    def rot(x):
        x1, x2 = x[:half], x[half:]
        return jnp.concatenate([x1 * cos - x2 * sin, x2 * cos + x1 * sin], axis=0)

    qc = rot(q_ref[...])
    kc = rot(k_ref[...]) * (RET_DK ** -0.5)
    v = v_ref[...]
    gr = g_ref[...]
    gn = gn_ref[...]
    for h in range(RET_HEADS):
        cols = slice(h * RET_DV, (h + 1) * RET_DV)
        new_state = s_in[h] * gam_ref[h] + kc[:, h:h + 1] * v[:, cols]
        s_out[h] = new_state
        o = jnp.sum(qc[:, h:h + 1] * new_state, axis=0, keepdims=True)
        o_ref[:, cols] = _group_norm_gate(o, gn[:, cols], gr[:, cols])


def _ret_sample(q_t, k_t, v, g, state_ret, layer, pos, gn, prev_out):
    s_cnt = q_t.shape[0]
    cos, sin = _rot_tables(jnp.full((1,), pos, dtype=jnp.int32))
    cos, sin = cos.reshape(-1, 1), sin.reshape(-1, 1)
    gam = _decay_tables(1)[3]
    col_spec = pl.BlockSpec((None, RET_DK, RET_HEADS), lambda s: (s, 0, 0))
    row_spec = pl.BlockSpec((None, 1, RET_V_W), lambda s: (s, 0, 0))
    rot_spec = pl.BlockSpec((RET_DK // 2, 1), lambda s: (0, 0))
    state_spec = pl.BlockSpec((None, None, RET_HEADS, RET_DK, RET_DV), lambda s: (layer, s, 0, 0, 0))
    in_specs = [pl.BlockSpec(memory_space=pltpu.SMEM), col_spec, col_spec, row_spec, row_spec, rot_spec, rot_spec,
                pl.BlockSpec((1, RET_V_W), lambda s: (0, 0)), state_spec]
    args = [gam, q_t, k_t, v, g, cos, sin, gn.reshape(1, RET_V_W), state_ret]
    aliases = {}
    if prev_out is not None:
        in_specs.append(pl.BlockSpec(memory_space=pl.ANY))
        args.append(prev_out)
        aliases = {len(args) - 1: 1}
    return pl.pallas_call(
        _ret_sample_kernel,
        out_shape=[jax.ShapeDtypeStruct((s_cnt, 1, RET_V_W), F32),
                   jax.ShapeDtypeStruct(state_ret.shape, state_ret.dtype)],
        grid=(s_cnt,),
        in_specs=in_specs,
        out_specs=[row_spec, state_spec],
        input_output_aliases=aliases,
        compiler_params=_params(1),
        name="retention_sample",
    )(*args)


def _router_kernel(h_ref, w_ref, b_ref, idx_ref, wt_ref, cnt_ref):
    logits = jnp.dot(h_ref[...], w_ref[...], precision=lax.Precision.HIGHEST, preferred_element_type=F32)
    aff = jax.nn.sigmoid(logits)
    lane = lax.broadcasted_iota(jnp.int32, aff.shape, 1)
    sel, picks = _select_topk(aff + b_ref[...], lane, TOP_K, N_EXPERTS)
    out_lane = lax.broadcasted_iota(jnp.int32, idx_ref.shape, 1)
    idx = jnp.zeros(idx_ref.shape, jnp.int32)
    top_aff = jnp.zeros(idx_ref.shape, F32)
    for t, pick in enumerate(picks):
        e_t = jnp.sum(jnp.where(pick, lane, 0), axis=-1, keepdims=True)
        a_t = jnp.sum(jnp.where(pick, aff, 0.0), axis=-1, keepdims=True)
        idx = jnp.where(out_lane == t, e_t, idx)
        top_aff = jnp.where(out_lane == t, a_t, top_aff)
    idx_ref[...] = idx
    wt_ref[...] = top_aff / jnp.sum(top_aff, axis=-1, keepdims=True) * ROUTED_SCALE
    hist = jnp.sum(sel.astype(F32), axis=0, keepdims=True)
    cnt_ref[...] = jnp.broadcast_to(hist, cnt_ref.shape).astype(jnp.int32)


def _router(h, w_router_l, bias_l):
    m, d = h.shape
    tm = min(m, 512)
    out = pl.BlockSpec((tm, LANES), lambda i: (i, 0))
    idx, wt, cnt = pl.pallas_call(
        _router_kernel,
        out_shape=[jax.ShapeDtypeStruct((m, LANES), jnp.int32), jax.ShapeDtypeStruct((m, LANES), F32),
                   jax.ShapeDtypeStruct((m // tm, SUBLANES, N_EXPERTS), jnp.int32)],
        grid=(m // tm,),
        in_specs=[pl.BlockSpec((tm, d), lambda i: (i, 0)),
                  pl.BlockSpec((d, N_EXPERTS), lambda i: (0, 0)),
                  pl.BlockSpec((1, N_EXPERTS), lambda i: (0, 0))],
        out_specs=[out, out, pl.BlockSpec((None, SUBLANES, N_EXPERTS), lambda i: (i, 0, 0))],
        compiler_params=_params(1),
        name="router",
    )(h, w_router_l, bias_l.reshape(1, N_EXPERTS))
    return idx, wt, jnp.sum(cnt[:, 0, :], axis=0)


def _swiglu_bf16(x, wg, wu, wd):
    g = jnp.dot(x, wg, preferred_element_type=F32)
    u = jnp.dot(x, wu, preferred_element_type=F32)
    return jnp.dot((_silu(g) * u).astype(BF16), wd, preferred_element_type=F32)


def _shared_kernel(h_ref, wg_ref, wu_ref, wd_ref, o_ref, wgb, wub, wdb):
    @pl.when(pl.program_id(0) == 0)
    def _():
        wgb[...] = wg_ref[...].astype(BF16)
        wub[...] = wu_ref[...].astype(BF16)
        wdb[...] = wd_ref[...].astype(BF16)

    o_ref[...] = _swiglu_bf16(h_ref[...].astype(BF16), wgb[...], wub[...], wdb[...])


def _shared_expert(h, wg, wu, wd, layer):
    m, d = h.shape
    tm = min(m, 512)
    return pl.pallas_call(
        _shared_kernel,
        out_shape=jax.ShapeDtypeStruct((m, d), F32),
        grid=(m // tm,),
        in_specs=[pl.BlockSpec((tm, d), lambda i: (i, 0)),
                  pl.BlockSpec((None, d, D_SHARED), lambda i: (layer, 0, 0)),
                  pl.BlockSpec((None, d, D_SHARED), lambda i: (layer, 0, 0)),
                  pl.BlockSpec((None, D_SHARED, d), lambda i: (layer, 0, 0))],
        out_specs=pl.BlockSpec((tm, d), lambda i: (i, 0)),
        scratch_shapes=[pltpu.VMEM((d, D_SHARED), BF16), pltpu.VMEM((d, D_SHARED), BF16),
                        pltpu.VMEM((D_SHARED, d), BF16)],
        compiler_params=_params(1),
        name="shared_expert",
    )(h, wg, wu, wd)


def _experts_kernel(blk_e, src, dst, n_used, h_hbm, wrow_ref, wg_ref, wu_ref, wd_ref, y_hbm,
                    xs, ys, wgb, wub, wdb, gsem, ssem):
    i = pl.program_id(0)
    n_act = n_used[0]
    last = pl.num_programs(0) - 1
    groups = MOE_ROWS // SUBLANES

    def gather(blk, slot, r, src_row):
        return pltpu.make_async_copy(h_hbm.at[src_row], xs.at[slot, r // SUBLANES, :, r % SUBLANES, :],
                                     gsem.at[slot])

    def scatter(blk, slot, r, dst_row):
        return pltpu.make_async_copy(ys.at[slot, r // SUBLANES, :, r % SUBLANES, :], y_hbm.at[dst_row],
                                     ssem.at[slot])

    def start_gathers(blk, slot):
        for r in range(MOE_ROWS):
            gather(blk, slot, r, src[blk * MOE_ROWS + r]).start()

    def wait_gathers(slot):
        for r in range(MOE_ROWS):
            gather(0, slot, r, 0).wait()

    def start_scatters(blk, slot):
        for r in range(MOE_ROWS):
            scatter(blk, slot, r, dst[blk * MOE_ROWS + r]).start()

    def wait_scatters(slot):
        for r in range(MOE_ROWS):
            scatter(0, slot, r, 0).wait()

    @pl.when(i < n_act)
    def _():
        slot = i % 2
        nxt = jnp.minimum(i + 1, last)

        @pl.when(i == 0)
        def _():
            start_gathers(0, 0)

        @pl.when((i == 0) | (blk_e[i] != blk_e[jnp.maximum(i - 1, 0)]))
        def _():
            wgb[...] = wg_ref[...].astype(BF16)
            wub[...] = wu_ref[...].astype(BF16)
            wdb[...] = wd_ref[...].astype(BF16)

        wait_gathers(slot)

        @pl.when(i >= 2)
        def _():
            wait_scatters(slot)

        start_gathers(nxt, 1 - slot)
        x = jnp.concatenate([xs[slot, :, t].reshape(MOE_ROWS, LANES) for t in range(ROW_TILES)], axis=1)
        y = _swiglu_bf16(x.astype(BF16), wgb[...], wub[...], wdb[...]) * wrow_ref[...]
        for t in range(ROW_TILES):
            ys[slot, :, t] = y[:, t * LANES:(t + 1) * LANES].reshape(groups, SUBLANES, LANES)
        start_scatters(i, slot)

        @pl.when(i == n_act - 1)
        def _():
            wait_gathers(1 - slot)
            wait_scatters(slot)

            @pl.when(i >= 1)
            def _():
                wait_scatters(1 - slot)


def _routed_experts(h_rows, top_idx, top_w, counts, wg, wu, wd, layer, n_out_rows):
    m = h_rows.shape[0]
    a = m * TOP_K
    n_blk = -(-(a + N_EXPERTS * (MOE_ROWS - 1)) // MOE_ROWS)
    n_rows = n_blk * MOE_ROWS
    flat_e = top_idx.reshape(a)
    order = jnp.argsort(flat_e).astype(jnp.int32)
    start = jnp.cumsum(counts) - counts
    padded = (counts + MOE_ROWS - 1) // MOE_ROWS * MOE_ROWS
    ends = jnp.cumsum(padded)
    pstart = ends - padded
    blk_e = jnp.minimum(jnp.searchsorted(ends, jnp.arange(n_blk, dtype=jnp.int32) * MOE_ROWS, side='right'),
                        N_EXPERTS - 1).astype(jnp.int32)
    n_used = (ends[-1:] // MOE_ROWS).astype(jnp.int32)
    pos = jnp.arange(n_rows, dtype=jnp.int32)
    per_row = lambda per_blk: jnp.broadcast_to(per_blk[:, None], (n_blk, MOE_ROWS)).reshape(n_rows)
    rank = pos - per_row(pstart[blk_e])
    valid = rank < per_row(counts[blk_e])
    asg = order[jnp.clip(per_row(start[blk_e]) + rank, 0, a - 1)]
    src = jnp.where(valid, asg // TOP_K, 0)
    dst = jnp.where(valid, asg, a + pos % (2 * MOE_ROWS))
    wrow = jnp.where(valid, top_w.reshape(a)[asg], 0.0).reshape(n_rows, 1)

    d = D_MODEL
    groups = MOE_ROWS // SUBLANES
    wspec = lambda i, be, s, t, n: (layer, be[i], 0, 0)
    grid_spec = pltpu.PrefetchScalarGridSpec(
        num_scalar_prefetch=4,
        grid=(n_blk,),
        in_specs=[pl.BlockSpec(memory_space=pl.ANY),
                  pl.BlockSpec((MOE_ROWS, 1), lambda i, be, s, t, n: (i, 0)),
                  pl.BlockSpec((None, None, d, D_EXPERT), wspec),
                  pl.BlockSpec((None, None, d, D_EXPERT), wspec),
                  pl.BlockSpec((None, None, D_EXPERT, d), wspec)],
        out_specs=pl.BlockSpec(memory_space=pl.ANY),
        scratch_shapes=[pltpu.VMEM((2, groups, ROW_TILES, SUBLANES, LANES), F32),
                        pltpu.VMEM((2, groups, ROW_TILES, SUBLANES, LANES), F32),
                        pltpu.VMEM((d, D_EXPERT), BF16), pltpu.VMEM((d, D_EXPERT), BF16),
                        pltpu.VMEM((D_EXPERT, d), BF16),
                        pltpu.SemaphoreType.DMA((2,)), pltpu.SemaphoreType.DMA((2,))],
    )
    return pl.pallas_call(
        _experts_kernel,
        out_shape=jax.ShapeDtypeStruct((n_out_rows, ROW_TILES, LANES), F32),
        grid_spec=grid_spec,
        compiler_params=_params(1),
        name="routed_experts",
    )(blk_e, src, dst, n_used, h_rows, wrow, wg, wu, wd)


def _combine_kernel(alpha, has_mod, y_ref, sh_ref, x_ref, gt_ref, g_ref, b_ref, *rest):
    if has_mod:
        sc_ref, shift_ref, xo_ref, ho_ref, slab_s, z_s = rest
    else:
        xo_ref, slab_s, z_s = rest
    routed = y_ref[:, 0]
    for k in range(1, TOP_K):
        routed = routed + y_ref[:, k]
    slab_s[...] = routed
    for t in range(ROW_TILES):
        cols = slice(t * LANES, (t + 1) * LANES)
        z_s[:, cols] = sh_ref[:, cols] + slab_s[:, t, :]
    z = alpha * x_ref[...] + gt_ref[...] * z_s[...]
    mu = jnp.mean(z, axis=-1, keepdims=True)
    dz = z - mu
    var = jnp.mean(dz * dz, axis=-1, keepdims=True)
    xn = dz * lax.rsqrt(var + LN_EPS) * g_ref[...] + b_ref[...]
    xo_ref[...] = xn
    if has_mod:
        ho_ref[...] = (xn * (1.0 + sc_ref[...]) + shift_ref[...]).astype(ho_ref.dtype)


def _combine_ln(alpha, y_rows, shared, x, gate, ln_g, ln_b, tok0, sc=None, sh=None):
    g, r, d = x.shape
    tr = min(r, 128)
    per_row = gate.shape[1] == r and r > 1
    has_mod = sc is not None
    nt = r // tr
    t0 = tok0 // tr
    tile = pl.BlockSpec((None, tr, d), lambda g, i: (g, i, 0))
    flat = pl.BlockSpec((tr, d), lambda g, i: (g * nt + i, 0))
    vec = pl.BlockSpec((1, d), lambda g, i: (0, 0))
    in_specs = [pl.BlockSpec((tr, TOP_K, ROW_TILES, LANES), lambda g, i: (t0 + g * nt + i, 0, 0, 0)),
                flat, tile, _row_spec(tr, d, per_row), vec, vec]
    args = [y_rows, shared, x, gate, ln_g.reshape(1, d), ln_b.reshape(1, d)]
    out_shape = [jax.ShapeDtypeStruct((g, r, d), F32)]
    out_specs = [tile]
    if has_mod:
        in_specs += [_row_spec(tr, d, per_row), _row_spec(tr, d, per_row)]
        args += [sc, sh]
        out_shape.append(jax.ShapeDtypeStruct((g, r, d), BF16))
        out_specs.append(tile)
    res = pl.pallas_call(
        functools.partial(_combine_kernel, alpha, has_mod),
        out_shape=out_shape,
        grid=(g, nt),
        in_specs=in_specs,
        out_specs=out_specs,
        scratch_shapes=[pltpu.VMEM((tr, ROW_TILES, LANES), F32), pltpu.VMEM((tr, d), F32)],
        compiler_params=_params(2),
        name="moe_combine_ln",
    )(*args)
    return res if has_mod else (res[0], None)


def kernel(x_prompt, x_sample, cache_k, cache_v, state_ret, page_table, c_prompt, c_sample, w_ada, b_ada, w_in, w_attn_br, ret_norm_g, w_ret_br, w_o, ln_g, ln_b, w_router, router_bias, w_exp_gate, w_exp_up, w_exp_down, w_sh_gate, w_sh_up, w_sh_down):
    bsz, seq, d = x_prompt.shape
    s_cnt = x_sample.shape[0]
    depth = w_in.shape[0]
    alpha = (2.0 * depth) ** 0.25
    page_rows = cache_k.shape[2]
    past = page_table.shape[1] * page_rows
    m_p = bsz * seq
    m_all = m_p + s_cnt
    y_tokens = m_all + -(-2 * MOE_ROWS // TOP_K)

    n_c = bsz + s_cnt
    r_c = -(-n_c // SUBLANES) * SUBLANES
    c_all = jnp.concatenate([c_prompt, c_sample, jnp.zeros((r_c - n_c, d), F32)], axis=0)
    mod = _adaln(c_all, w_ada, b_ada)
    mod_p = mod[:, :bsz].reshape(depth, bsz, N_MOD, 1, d)
    mod_s = mod[:, bsz:n_c].reshape(depth, 1, s_cnt, N_MOD, d)

    def mods(layer):
        mp = [mod_p[layer, :, j] for j in range(N_MOD)]
        ms = [mod_s[layer, :, :, j] for j in range(N_MOD)]
        return mp, ms

    pt_flat = page_table.reshape(-1)

    xp = x_prompt
    xs = x_sample.reshape(1, s_cnt, d)
    mp, ms = mods(0)
    hp = _modulate(xp, mp[1], mp[0], BF16)
    hs = _modulate(xs, ms[1], ms[0], BF16)

    kp, vp, sp, kd, vd = [], [], [], [], []
    state_out = None
    for layer in range(depth):
        mp, ms = mods(layer)
        pp = _mm(hp.reshape(m_p, d), w_in, layer, tn=1024, out_dtype=F32)
        o_a = _moba_prompt(pp, bsz, seq)
        o_r, s_new = _ret_prompt(pp, bsz, seq, ret_norm_g[layer])
        t1 = _mm(o_a, w_attn_br, layer, tn=512, out_dtype=F32, gate=pp, gate_col=OFF_BA)
        mixed = _mm(o_r, w_ret_br, layer, tn=512, out_dtype=BF16, gate=pp, gate_col=OFF_BR, add=t1)
        y_p = _mm(mixed, w_o, layer, tn=512, out_dtype=F32).reshape(bsz, seq, d)
        xp, h2p, hr_p = _post_ln(alpha, xp, y_p, mp[2], ln_g[layer, 0], ln_b[layer, 0], mp[4], mp[3])
        kp.append(pp[:, OFF_KA:OFF_VA].reshape(bsz, seq, KV_HEADS, HEAD_DIM))
        vp.append(pp[:, OFF_VA:OFF_QR].reshape(bsz, seq, KV_HEADS, HEAD_DIM))
        sp.append(s_new)

        ps = _mm(hs.reshape(s_cnt, d), w_in, layer, tn=1024, out_dtype=F32)
        k_new = ps[:, OFF_KA:OFF_VA].reshape(s_cnt, KV_HEADS, HEAD_DIM)
        v_new = ps[:, OFF_VA:OFF_QR].reshape(s_cnt, KV_HEADS, HEAD_DIM)
        o_as = _moba_sample(ps[:, OFF_QA:OFF_KA].reshape(s_cnt, N_HEADS, HEAD_DIM),
                            jnp.repeat(k_new, Q_PER_KV, axis=1), jnp.repeat(v_new, Q_PER_KV, axis=1),
                            cache_k, cache_v, pt_flat, layer)
        q_t = ps[:, OFF_QR:OFF_KR].reshape(s_cnt, RET_HEADS, RET_DK).transpose(0, 2, 1)
        k_t = ps[:, OFF_KR:OFF_VR].reshape(s_cnt, RET_HEADS, RET_DK).transpose(0, 2, 1)
        o_rs, state_out = _ret_sample(q_t, k_t, ps[:, OFF_VR:OFF_GR].reshape(s_cnt, 1, RET_V_W),
                                      ps[:, OFF_GR:OFF_BA].reshape(s_cnt, 1, RET_V_W),
                                      state_ret, layer, past, ret_norm_g[layer], state_out)
        t1s = _mm(o_as.reshape(s_cnt, ATTN_Q_W).astype(BF16), w_attn_br, layer, tn=512, out_dtype=F32,
                  gate=ps, gate_col=OFF_BA)
        mixed_s = _mm(o_rs.reshape(s_cnt, RET_V_W).astype(BF16), w_ret_br, layer, tn=512, out_dtype=BF16,
                      gate=ps, gate_col=OFF_BR, add=t1s)
        y_s = _mm(mixed_s, w_o, layer, tn=512, out_dtype=F32).reshape(1, s_cnt, d)
        xs, h2s, hr_s = _post_ln(alpha, xs, y_s, ms[2], ln_g[layer, 0], ln_b[layer, 0], ms[4], ms[3])
        kd.append(k_new.reshape(s_cnt, 1, KV_HEADS, HEAD_DIM))
        vd.append(v_new.reshape(s_cnt, 1, KV_HEADS, HEAD_DIM))

        h2p = h2p.reshape(m_p, d)
        h2s = h2s.reshape(s_cnt, d)
        idx_p, wt_p, cnt_p = _router(h2p, w_router[layer], router_bias[layer])
        idx_s, wt_s, cnt_s = _router(h2s, w_router[layer], router_bias[layer])
        sh_p = _shared_expert(h2p, w_sh_gate, w_sh_up, w_sh_down, layer)
        sh_s = _shared_expert(h2s, w_sh_gate, w_sh_up, w_sh_down, layer)
        h_rows = jnp.concatenate([hr_p, hr_s], axis=0)
        top_idx = jnp.concatenate([idx_p[:, :TOP_K], idx_s[:, :TOP_K]], axis=0)
        top_w = jnp.concatenate([wt_p[:, :TOP_K], wt_s[:, :TOP_K]], axis=0)
        y_rows = _routed_experts(h_rows, top_idx, top_w, cnt_p + cnt_s, w_exp_gate, w_exp_up, w_exp_down, layer,
                                 y_tokens * TOP_K)
        y_rows = y_rows.reshape(y_tokens, TOP_K, ROW_TILES, LANES)
        if layer + 1 < depth:
            mp_n, ms_n = mods(layer + 1)
            nxt_p, nxt_s = (mp_n[1], mp_n[0]), (ms_n[1], ms_n[0])
        else:
            nxt_p = nxt_s = (None, None)
        xp, hp = _combine_ln(alpha, y_rows, sh_p, xp, mp[5], ln_g[layer, 1], ln_b[layer, 1], 0, *nxt_p)
        xs, hs = _combine_ln(alpha, y_rows, sh_s, xs, ms[5], ln_g[layer, 1], ln_b[layer, 1], m_p, *nxt_s)

    return (xp, xs.reshape(s_cnt, 1, d), jnp.stack(kp), jnp.stack(vp), jnp.stack(sp).astype(state_ret.dtype),
            jnp.stack(kd), jnp.stack(vd), state_out)
```

```python
import functools

import jax
import jax.numpy as jnp
import numpy as np
from jax import lax
from jax.experimental import pallas as pl
from jax.experimental.pallas import tpu as pltpu

F32 = jnp.float32
BF16 = jnp.bfloat16

D_MODEL = 2048
N_HEADS = 16
KV_HEADS = 8
HEAD_DIM = 128
Q_PER_KV = N_HEADS // KV_HEADS
MOBA_BLOCK = 256
MOBA_TOPK = 3
RET_HEADS = 8
RET_DK = D_MODEL // RET_HEADS
RET_DV = 2 * D_MODEL // RET_HEADS
RET_CHUNK = 128
ROT_BASE = 10000.0
N_EXPERTS = 64
TOP_K = 6
D_EXPERT = D_MODEL // 4
D_SHARED = D_MODEL // 4
ROUTED_SCALE = 2.5
LN_EPS = 1e-5
N_MOD = 6

ATTN_Q_W = N_HEADS * HEAD_DIM
ATTN_KV_W = KV_HEADS * HEAD_DIM
RET_QK_W = RET_HEADS * RET_DK
RET_V_W = RET_HEADS * RET_DV
OFF_QA = 0
OFF_KA = OFF_QA + ATTN_Q_W
OFF_VA = OFF_KA + ATTN_KV_W
OFF_QR = OFF_VA + ATTN_KV_W
OFF_KR = OFF_QR + RET_QK_W
OFF_VR = OFF_KR + RET_QK_W
OFF_GR = OFF_VR + RET_V_W
OFF_BA = OFF_GR + RET_V_W
OFF_BR = OFF_BA + D_MODEL
D_IN = OFF_BR + D_MODEL

LANES = 128
SUBLANES = 8
VMEM_LIMIT_BYTES = 52 * 1024 * 1024
MOE_ROWS = 256
ROW_TILES = D_MODEL // LANES

NEG_INF = float("-inf")


def _params(n_axes):
    return pltpu.CompilerParams(dimension_semantics=("arbitrary",) * n_axes,
                                vmem_limit_bytes=VMEM_LIMIT_BYTES)


def _silu(x):
    return x * jax.nn.sigmoid(x)


def _ada_kernel(c_ref, w_ref, b_ref, o_ref):
    a = _silu(c_ref[...]).astype(BF16)
    o_ref[...] = jnp.dot(a, w_ref[...].astype(BF16), preferred_element_type=F32) + b_ref[...]


def _adaln(c_all, w_ada, b_ada):
    depth, d, n = w_ada.shape
    r = c_all.shape[0]
    tn = 1024
    return pl.pallas_call(
        _ada_kernel,
        out_shape=jax.ShapeDtypeStruct((depth, r, n), F32),
        grid=(depth, n // tn),
        in_specs=[pl.BlockSpec((r, d), lambda l, j: (0, 0)),
                  pl.BlockSpec((None, d, tn), lambda l, j: (l, 0, j)),
                  pl.BlockSpec((None, 1, tn), lambda l, j: (l, 0, j))],
        out_specs=pl.BlockSpec((None, r, tn), lambda l, j: (l, 0, j)),
        compiler_params=_params(2),
        name="adaln",
    )(c_all, w_ada, b_ada.reshape(depth, 1, n))


def _row_spec(tr, d, per_row):
    if per_row:
        return pl.BlockSpec((None, tr, d), lambda g, i: (g, i, 0))
    return pl.BlockSpec((None, 1, d), lambda g, i: (g, 0, 0))


def _modulate_kernel(x_ref, sc_ref, sh_ref, o_ref):
    o_ref[...] = (x_ref[...] * (1.0 + sc_ref[...]) + sh_ref[...]).astype(o_ref.dtype)


def _modulate(x, sc, sh, out_dtype):
    g, r, d = x.shape
    tr = min(r, 512)
    per_row = sc.shape[1] == r and r > 1
    return pl.pallas_call(
        _modulate_kernel,
        out_shape=jax.ShapeDtypeStruct((g, r, d), out_dtype),
        grid=(g, r // tr),
        in_specs=[pl.BlockSpec((None, tr, d), lambda g, i: (g, i, 0)),
                  _row_spec(tr, d, per_row), _row_spec(tr, d, per_row)],
        out_specs=pl.BlockSpec((None, tr, d), lambda g, i: (g, i, 0)),
        compiler_params=_params(2),
        name="modulate",
    )(x, sc, sh)


def _ln_kernel(alpha, has_mod, x_ref, y_ref, gt_ref, g_ref, b_ref, *rest):
    if has_mod:
        sc_ref, sh_ref, xo_ref, ho_ref, hr_ref = rest
    else:
        (xo_ref,) = rest
    z = alpha * x_ref[...] + gt_ref[...] * y_ref[...]
    mu = jnp.mean(z, axis=-1, keepdims=True)
    dz = z - mu
    var = jnp.mean(dz * dz, axis=-1, keepdims=True)
    xn = dz * lax.rsqrt(var + LN_EPS) * g_ref[...] + b_ref[...]
    xo_ref[...] = xn
    if has_mod:
        h = xn * (1.0 + sc_ref[...]) + sh_ref[...]
        ho_ref[...] = h
        for t in range(ROW_TILES):
            hr_ref[:, t, :] = h[:, t * LANES:(t + 1) * LANES]


def _post_ln(alpha, x, y, gate, ln_g, ln_b, sc=None, sh=None):
    g, r, d = x.shape
    tr = min(r, 256)
    per_row = gate.shape[1] == r and r > 1
    has_mod = sc is not None
    tile = pl.BlockSpec((None, tr, d), lambda g, i: (g, i, 0))
    vec = pl.BlockSpec((1, d), lambda g, i: (0, 0))
    in_specs = [tile, tile, _row_spec(tr, d, per_row), vec, vec]
    args = [x, y, gate, ln_g.reshape(1, d), ln_b.reshape(1, d)]
    out_shape = [jax.ShapeDtypeStruct((g, r, d), F32)]
    out_specs = [tile]
    if has_mod:
        in_specs += [_row_spec(tr, d, per_row), _row_spec(tr, d, per_row)]
        args += [sc, sh]
        nt = r // tr
        out_shape += [jax.ShapeDtypeStruct((g, r, d), F32), jax.ShapeDtypeStruct((g * r, ROW_TILES, LANES), F32)]
        out_specs += [tile, pl.BlockSpec((tr, ROW_TILES, LANES), lambda g, i: (g * nt + i, 0, 0))]
    res = pl.pallas_call(
        functools.partial(_ln_kernel, alpha, has_mod),
        out_shape=out_shape,
        grid=(g, r // tr),
        in_specs=in_specs,
        out_specs=out_specs,
        compiler_params=_params(2),
        name="post_ln",
    )(*args)
    return res if has_mod else (res[0], None, None)


def _mm_kernel(has_gate, has_add, a_ref, w_ref, *rest):
    rest = list(rest)
    gate_ref = rest.pop(0) if has_gate else None
    add_ref = rest.pop(0) if has_add else None
    o_ref, wbf = rest

    @pl.when(pl.program_id(1) == 0)
    def _():
        wbf[...] = w_ref[...].astype(BF16)

    acc = jnp.dot(a_ref[...], wbf[...], preferred_element_type=F32)
    if has_gate:
        acc = acc * jax.nn.sigmoid(gate_ref[...])
    if has_add:
        acc = acc + add_ref[...]
    o_ref[...] = acc.astype(o_ref.dtype)


def _mm(a, w, layer, *, tn, out_dtype, gate=None, gate_col=0, add=None):
    m, k = a.shape
    n = w.shape[2]
    tm = min(m, 512)
    in_specs = [pl.BlockSpec((tm, k), lambda j, i: (i, 0)),
                pl.BlockSpec((None, k, tn), lambda j, i: (layer, 0, j))]
    args = [a, w]
    if gate is not None:
        goff = gate_col // tn
        in_specs.append(pl.BlockSpec((tm, tn), lambda j, i: (i, goff + j)))
        args.append(gate)
    if add is not None:
        in_specs.append(pl.BlockSpec((tm, tn), lambda j, i: (i, j)))
        args.append(add)
    return pl.pallas_call(
        functools.partial(_mm_kernel, gate is not None, add is not None),
        out_shape=jax.ShapeDtypeStruct((m, n), out_dtype),
        grid=(n // tn, m // tm),
        in_specs=in_specs,
        out_specs=pl.BlockSpec((tm, tn), lambda j, i: (i, j)),
        scratch_shapes=[pltpu.VMEM((k, tn), BF16)],
        compiler_params=_params(2),
        name="matmul",
    )(*args)


def _select_topk(score, lane, k, n_lanes):
    sel = jnp.zeros(score.shape, dtype=jnp.bool_)
    picks = []
    for _ in range(k):
        m = jnp.max(score, axis=-1, keepdims=True)
        cand = jnp.where((score == m) & (m > NEG_INF), lane, n_lanes)
        idx = jnp.min(cand, axis=-1, keepdims=True)
        pick = lane == idx
        sel = sel | pick
        score = jnp.where(pick, NEG_INF, score)
        picks.append(pick)
    return sel, picks


def _select_topk_rows(score, row, k, n_rows):
    sel = jnp.zeros(score.shape, dtype=jnp.bool_)
    for _ in range(k):
        m = jnp.max(score, axis=0, keepdims=True)
        cand = jnp.where((score == m) & (m > NEG_INF), row, n_rows)
        idx = jnp.min(cand, axis=0, keepdims=True)
        pick = row == idx
        sel = sel | pick
        score = jnp.where(pick, NEG_INF, score)
    return sel


def _moba_prompt_kernel(nb, q_ref, k_ref, v_ref, o_ref, kbf, vtb, kmean, sel_s, acc_s):
    i = pl.program_id(2)
    scale = HEAD_DIM ** -0.5

    @pl.when(i == 0)
    def _():
        kf = k_ref[...]
        kbf[...] = kf.reshape(nb // 2, 2 * MOBA_BLOCK, HEAD_DIM).astype(BF16)
        kmean[...] = jnp.mean(kf.reshape(nb, MOBA_BLOCK, HEAD_DIM), axis=1)
        for j in range(nb // 2):
            vtb[j] = v_ref[j * 2 * MOBA_BLOCK:(j + 1) * 2 * MOBA_BLOCK, :].T.astype(BF16)

    q = q_ref[...]
    qt = jnp.concatenate([q[:, :HEAD_DIM].T, q[:, HEAD_DIM:].T], axis=1)
    width = qt.shape[1]
    gate = jnp.dot(kmean[...], qt, precision=lax.Precision.HIGHEST, preferred_element_type=F32)
    blk = lax.broadcasted_iota(jnp.int32, gate.shape, 0)
    gate = jnp.where(blk < i, gate, NEG_INF)
    sel_s[...] = _select_topk_rows(gate, blk, MOBA_TOPK, nb).astype(F32)

    qtb = qt.astype(BF16)

    own_off = pl.multiple_of((i % 2) * MOBA_BLOCK, MOBA_BLOCK)
    k_own = kbf[i // 2, pl.ds(own_off, MOBA_BLOCK), :]
    v_own = v_ref[pl.ds(pl.multiple_of(i * MOBA_BLOCK, MOBA_BLOCK), MOBA_BLOCK), :].T.astype(BF16)
    s = jnp.dot(k_own, qtb, preferred_element_type=F32) * scale
    kpos = lax.broadcasted_iota(jnp.int32, s.shape, 0)
    qpos = lax.broadcasted_iota(jnp.int32, s.shape, 1) % MOBA_BLOCK
    s = jnp.where(kpos <= qpos, s, NEG_INF)
    m0 = jnp.max(s, axis=0, keepdims=True)
    p = jnp.exp(s - m0)
    l0 = jnp.sum(p, axis=0, keepdims=True)
    acc_s[...] = jnp.dot(v_own, p.astype(BF16), preferred_element_type=F32)

    def body(t, carry):
        m, l = carry
        picked = jnp.concatenate(
            [jnp.broadcast_to(sel_s[pl.ds(2 * t, 1), :], (MOBA_BLOCK, width)),
             jnp.broadcast_to(sel_s[pl.ds(2 * t + 1, 1), :], (MOBA_BLOCK, width))], axis=0)
        s = jnp.dot(kbf[t], qtb, preferred_element_type=F32) * scale
        s = jnp.where(picked > 0.0, s, NEG_INF)
        m_new = jnp.maximum(m, jnp.max(s, axis=0, keepdims=True))
        a = jnp.exp(m - m_new)
        p = jnp.exp(s - m_new)
        l = a * l + jnp.sum(p, axis=0, keepdims=True)
        acc_s[...] = a * acc_s[...] + jnp.dot(vtb[t], p.astype(BF16), preferred_element_type=F32)
        return m_new, l

    _, l = lax.fori_loop(0, (i + 1) // 2, body, (m0, l0))
    ot = acc_s[...] / l
    half = width // 2
    o_ref[...] = jnp.concatenate([ot[:, :half].T, ot[:, half:].T], axis=1).astype(o_ref.dtype)


def _moba_prompt(p_all, bsz, seq):
    nb = seq // MOBA_BLOCK
    qw = Q_PER_KV * HEAD_DIM
    return pl.pallas_call(
        functools.partial(_moba_prompt_kernel, nb),
        out_shape=jax.ShapeDtypeStruct((bsz * seq, ATTN_Q_W), BF16),
        grid=(bsz, KV_HEADS, nb),
        in_specs=[pl.BlockSpec((MOBA_BLOCK, qw), lambda b, g, i: (b * nb + i, OFF_QA // qw + g)),
                  pl.BlockSpec((seq, HEAD_DIM), lambda b, g, i: (b, OFF_KA // HEAD_DIM + g)),
                  pl.BlockSpec((seq, HEAD_DIM), lambda b, g, i: (b, OFF_VA // HEAD_DIM + g))],
        out_specs=pl.BlockSpec((MOBA_BLOCK, qw), lambda b, g, i: (b * nb + i, g)),
        scratch_shapes=[pltpu.VMEM((nb // 2, 2 * MOBA_BLOCK, HEAD_DIM), BF16),
                        pltpu.VMEM((nb // 2, HEAD_DIM, 2 * MOBA_BLOCK), BF16),
                        pltpu.VMEM((nb, HEAD_DIM), F32), pltpu.VMEM((nb, Q_PER_KV * MOBA_BLOCK), F32),
                        pltpu.VMEM((HEAD_DIM, Q_PER_KV * MOBA_BLOCK), F32)],
        compiler_params=_params(3),
        name="moba_prompt",
    )(p_all, p_all, p_all)


def _moba_sample_kernel(n_blocks, pages_per_step, pt_ref, q_ref, kn_ref, vn_ref, *rest):
    k_pages = rest[:pages_per_step]
    v_pages = rest[pages_per_step:2 * pages_per_step]
    o_ref, gate_s, m_s, l_s, o_s = rest[2 * pages_per_step:]
    c = pl.program_id(1)
    scale = HEAD_DIM ** -0.5
    page_rows = k_pages[0].shape[0]
    pages_per_block = MOBA_BLOCK // page_rows
    bps = pages_per_step // pages_per_block

    q = q_ref[...]
    qb = q.astype(BF16)
    head = lax.broadcasted_iota(jnp.int32, (N_HEADS, 1), 0)
    lane = lax.broadcasted_iota(jnp.int32, q.shape, 1)

    @pl.when(c == 0)
    def _():
        gate_s[...] = jnp.zeros_like(gate_s)
        m_s[...] = jnp.zeros_like(m_s)
        l_s[...] = jnp.zeros_like(l_s)

    kv_of_col = lax.broadcasted_iota(jnp.int32, (N_HEADS, MOBA_BLOCK * KV_HEADS), 1) % KV_HEADS
    mine = kv_of_col == head // Q_PER_KV
    gcol = lax.broadcasted_iota(jnp.int32, (N_HEADS, KV_HEADS), 1)
    for b in range(bps):
        blk = c * bps + b
        pages = range(b * pages_per_block, (b + 1) * pages_per_block)
        k3 = jnp.concatenate([k_pages[t][...] for t in pages], axis=0)
        v3 = jnp.concatenate([v_pages[t][...] for t in pages], axis=0)
        k_flat = k3.reshape(MOBA_BLOCK * KV_HEADS, HEAD_DIM).astype(BF16)
        v_flat = v3.reshape(MOBA_BLOCK * KV_HEADS, HEAD_DIM).astype(BF16)
        s = lax.dot_general(qb, k_flat, (((1,), (1,)), ((), ())), preferred_element_type=F32) * scale
        s = jnp.where(mine, s, NEG_INF)
        mb = jnp.max(s, axis=-1, keepdims=True)
        pb = jnp.exp(s - mb)
        lb = jnp.sum(pb, axis=-1, keepdims=True)
        ob = jnp.dot(pb.astype(BF16), v_flat, preferred_element_type=F32)
        kmean = jnp.mean(k3, axis=0)
        gate_all = lax.dot_general(q, kmean, (((1,), (1,)), ((), ())),
                                   precision=lax.Precision.HIGHEST, preferred_element_type=F32)
        gt = jnp.sum(jnp.where(gcol == head // Q_PER_KV, gate_all, 0.0), axis=-1, keepdims=True)
        here = lane == blk
        gate_s[...] = jnp.where(here, gt, gate_s[...])
        m_s[...] = jnp.where(here, mb, m_s[...])
        l_s[...] = jnp.where(here, lb, l_s[...])
        o_s[blk] = ob

    @pl.when(c == pl.num_programs(1) - 1)
    def _():
        gate = jnp.where(lane < n_blocks, gate_s[...], NEG_INF)
        sel, _ = _select_topk(gate, lane, MOBA_TOPK, LANES)
        s_self = jnp.sum(q * kn_ref[...], axis=-1, keepdims=True) * scale
        m_all = m_s[...]
        m_fin = jnp.maximum(jnp.max(jnp.where(sel, m_all, NEG_INF), axis=-1, keepdims=True), s_self)
        w = jnp.where(sel, jnp.exp(m_all - m_fin), 0.0)
        w_self = jnp.exp(s_self - m_fin)
        denom = jnp.sum(w * l_s[...], axis=-1, keepdims=True) + w_self
        num = w_self * vn_ref[...]
        for b in range(n_blocks):
            wb = jnp.sum(jnp.where(lane == b, w, 0.0), axis=-1, keepdims=True)
            num = num + wb * o_s[b]
        o_ref[...] = num / denom


def _moba_sample(q, k_new, v_new, cache_k, cache_v, page_table, layer):
    s_cnt = q.shape[0]
    page_rows = cache_k.shape[2]
    n_pages = page_table.shape[0] // s_cnt
    n_blocks = n_pages * page_rows // MOBA_BLOCK
    pages_per_step = min(8, n_pages)
    steps = n_pages // pages_per_step

    def page_spec(t):
        return pl.BlockSpec((None, None, page_rows, KV_HEADS, HEAD_DIM),
                            lambda s, c, pt: (layer, pt[s * n_pages + c * pages_per_step + t], 0, 0, 0))

    head_spec = pl.BlockSpec((None, N_HEADS, HEAD_DIM), lambda s, c, pt: (s, 0, 0))
    grid_spec = pltpu.PrefetchScalarGridSpec(
        num_scalar_prefetch=1,
        grid=(s_cnt, steps),
        in_specs=[head_spec, head_spec, head_spec]
                 + [page_spec(t) for t in range(pages_per_step)] * 2,
        out_specs=head_spec,
        scratch_shapes=[pltpu.VMEM((N_HEADS, LANES), F32), pltpu.VMEM((N_HEADS, LANES), F32),
                        pltpu.VMEM((N_HEADS, LANES), F32), pltpu.VMEM((n_blocks, N_HEADS, HEAD_DIM), F32)],
    )
    return pl.pallas_call(
        functools.partial(_moba_sample_kernel, n_blocks, pages_per_step),
        out_shape=jax.ShapeDtypeStruct((s_cnt, N_HEADS, HEAD_DIM), F32),
        grid_spec=grid_spec,
        compiler_params=_params(2),
        name="moba_sample",
    )(page_table, q, k_new, v_new, *([cache_k] * pages_per_step), *([cache_v] * pages_per_step))


def _group_norm_gate(o, gn, gr):
    mu = jnp.mean(o, axis=-1, keepdims=True)
    do = o - mu
    var = jnp.mean(do * do, axis=-1, keepdims=True)
    return do * lax.rsqrt(var + LN_EPS) * gn * _silu(gr)


def _ret_prompt_kernel(heads, q_ref, k_ref, v_ref, g_ref, cos_ref, sin_ref, dmask_ref, qdec_ref, kdec_ref, sdec_ref,
                       gn_ref, o_ref, s_ref):
    @pl.when(pl.program_id(2) == 0)
    def _():
        s_ref[...] = jnp.zeros_like(s_ref)

    cos = cos_ref[...]
    sin = sin_ref[...]
    half = RET_DK // 2

    def rot(x):
        x1, x2 = x[:, :half], x[:, half:]
        return jnp.concatenate([x1 * cos - x2 * sin, x2 * cos + x1 * sin], axis=1)

    for h in range(heads):
        kc = slice(h * RET_DK, (h + 1) * RET_DK)
        vc = slice(h * RET_DV, (h + 1) * RET_DV)
        q = rot(q_ref[:, kc])
        k = rot(k_ref[:, kc]) * (RET_DK ** -0.5)
        qb = q.astype(BF16)
        vb = v_ref[:, vc].astype(BF16)
        state = s_ref[h]
        scores = lax.dot_general(qb, k.astype(BF16), (((1,), (1,)), ((), ())),
                                 preferred_element_type=F32) * dmask_ref[h]
        o = jnp.dot(scores.astype(BF16), vb, preferred_element_type=F32)
        o = o + jnp.dot(qb, state.astype(BF16), preferred_element_type=F32) * qdec_ref[h]
        kd = (k * kdec_ref[h]).astype(BF16)
        s_ref[h] = state * sdec_ref[h] + lax.dot_general(kd, vb, (((0,), (0,)), ((), ())),
                                                         preferred_element_type=F32)
        o_ref[:, vc] = _group_norm_gate(o, gn_ref[:, vc], g_ref[:, vc]).astype(o_ref.dtype)


def _decay_tables(c):
    lg = jnp.log1p(-jnp.exp2(-5.0 - jnp.arange(RET_HEADS, dtype=F32)))
    i = jnp.arange(c, dtype=F32)
    diff = i[:, None] - i[None, :]
    causal = diff >= 0.0
    dmask = jnp.where(causal[None], jnp.exp(jnp.where(causal, diff, 0.0)[None] * lg[:, None, None]), 0.0)
    qdec = jnp.exp((i[:, None] + 1.0) * lg[None, :]).T[:, :, None]
    kdec = jnp.exp((c - 1.0 - i)[:, None] * lg[None, :]).T[:, :, None]
    sdec = jnp.exp(c * lg)
    return dmask, qdec, kdec, sdec


def _rot_tables(pos):
    half = RET_DK // 2
    theta = ROT_BASE ** -jnp.linspace(0.0, 1.0, half, dtype=F32)
    ang = pos.astype(F32)[:, None] * theta[None, :]
    return jnp.cos(ang), jnp.sin(ang)


def _ret_prompt(p_all, bsz, seq, gn):
    c = min(RET_CHUNK, seq)
    nc = seq // c
    hps = 4
    cos, sin = _rot_tables(jnp.arange(seq, dtype=jnp.int32))
    dmask, qdec, kdec, sdec = _decay_tables(c)
    kw, vw = hps * RET_DK, hps * RET_DV
    head_vec = lambda b, h, j: (h, 0, 0)
    return pl.pallas_call(
        functools.partial(_ret_prompt_kernel, hps),
        out_shape=[jax.ShapeDtypeStruct((bsz * seq, RET_V_W), BF16),
                   jax.ShapeDtypeStruct((bsz, RET_HEADS, RET_DK, RET_DV), F32)],
        grid=(bsz, RET_HEADS // hps, nc),
        in_specs=[pl.BlockSpec((c, kw), lambda b, h, j: (b * nc + j, OFF_QR // kw + h)),
                  pl.BlockSpec((c, kw), lambda b, h, j: (b * nc + j, OFF_KR // kw + h)),
                  pl.BlockSpec((c, vw), lambda b, h, j: (b * nc + j, OFF_VR // vw + h)),
                  pl.BlockSpec((c, vw), lambda b, h, j: (b * nc + j, OFF_GR // vw + h)),
                  pl.BlockSpec((c, RET_DK // 2), lambda b, h, j: (j, 0)),
                  pl.BlockSpec((c, RET_DK // 2), lambda b, h, j: (j, 0)),
                  pl.BlockSpec((hps, c, c), head_vec),
                  pl.BlockSpec((hps, c, 1), head_vec),
                  pl.BlockSpec((hps, c, 1), head_vec),
                  pl.BlockSpec((hps, 1, 1), head_vec),
                  pl.BlockSpec((1, vw), lambda b, h, j: (0, h))],
        out_specs=[pl.BlockSpec((c, vw), lambda b, h, j: (b * nc + j, h)),
                   pl.BlockSpec((None, hps, RET_DK, RET_DV), lambda b, h, j: (b, h, 0, 0))],
        compiler_params=_params(3),
        name="retention_prompt",
    )(p_all, p_all, p_all, p_all, cos, sin, dmask, qdec, kdec, sdec.reshape(RET_HEADS, 1, 1),
      gn.reshape(1, RET_V_W))


def _ret_sample_kernel(gam_ref, q_ref, k_ref, v_ref, g_ref, cos_ref, sin_ref, gn_ref, s_in, *rest):
    o_ref, s_out = rest[-2:]
    cos = cos_ref[...]
    sin = sin_ref[...]
    half = RET_DK // 2

    def rot(x):
        x1, x2 = x[:half], x[half:]
        return jnp.concatenate([x1 * cos - x2 * sin, x2 * cos + x1 * sin], axis=0)

    qc = rot(q_ref[...])
    kc = rot(k_ref[...]) * (RET_DK ** -0.5)
    v = v_ref[...]
    gr = g_ref[...]
    gn = gn_ref[...]
    for h in range(RET_HEADS):
        cols = slice(h * RET_DV, (h + 1) * RET_DV)
        new_state = s_in[h] * gam_ref[h] + kc[:, h:h + 1] * v[:, cols]
        s_out[h] = new_state
        o = jnp.sum(qc[:, h:h + 1] * new_state, axis=0, keepdims=True)
        o_ref[:, cols] = _group_norm_gate(o, gn[:, cols], gr[:, cols])


def _ret_sample(q_t, k_t, v, g, state_ret, layer, pos, gn, prev_out):
    s_cnt = q_t.shape[0]
    cos, sin = _rot_tables(jnp.full((1,), pos, dtype=jnp.int32))
    cos, sin = cos.reshape(-1, 1), sin.reshape(-1, 1)
    gam = _decay_tables(1)[3]
    col_spec = pl.BlockSpec((None, RET_DK, RET_HEADS), lambda s: (s, 0, 0))
    row_spec = pl.BlockSpec((None, 1, RET_V_W), lambda s: (s, 0, 0))
    rot_spec = pl.BlockSpec((RET_DK // 2, 1), lambda s: (0, 0))
    state_spec = pl.BlockSpec((None, None, RET_HEADS, RET_DK, RET_DV), lambda s: (layer, s, 0, 0, 0))
    in_specs = [pl.BlockSpec(memory_space=pltpu.SMEM), col_spec, col_spec, row_spec, row_spec, rot_spec, rot_spec,
                pl.BlockSpec((1, RET_V_W), lambda s: (0, 0)), state_spec]
    args = [gam, q_t, k_t, v, g, cos, sin, gn.reshape(1, RET_V_W), state_ret]
    aliases = {}
    if prev_out is not None:
        in_specs.append(pl.BlockSpec(memory_space=pl.ANY))
        args.append(prev_out)
        aliases = {len(args) - 1: 1}
    return pl.pallas_call(
        _ret_sample_kernel,
        out_shape=[jax.ShapeDtypeStruct((s_cnt, 1, RET_V_W), F32),
                   jax.ShapeDtypeStruct(state_ret.shape, state_ret.dtype)],
        grid=(s_cnt,),
        in_specs=in_specs,
        out_specs=[row_spec, state_spec],
        input_output_aliases=aliases,
        compiler_params=_params(1),
        name="retention_sample",
    )(*args)


def _router_kernel(h_ref, w_ref, b_ref, idx_ref, wt_ref, cnt_ref):
    logits = jnp.dot(h_ref[...], w_ref[...], precision=lax.Precision.HIGHEST, preferred_element_type=F32)
    aff = jax.nn.sigmoid(logits)
    lane = lax.broadcasted_iota(jnp.int32, aff.shape, 1)
    sel, picks = _select_topk(aff + b_ref[...], lane, TOP_K, N_EXPERTS)
    out_lane = lax.broadcasted_iota(jnp.int32, idx_ref.shape, 1)
    idx = jnp.zeros(idx_ref.shape, jnp.int32)
    top_aff = jnp.zeros(idx_ref.shape, F32)
    for t, pick in enumerate(picks):
        e_t = jnp.sum(jnp.where(pick, lane, 0), axis=-1, keepdims=True)
        a_t = jnp.sum(jnp.where(pick, aff, 0.0), axis=-1, keepdims=True)
        idx = jnp.where(out_lane == t, e_t, idx)
        top_aff = jnp.where(out_lane == t, a_t, top_aff)
    idx_ref[...] = idx
    wt_ref[...] = top_aff / jnp.sum(top_aff, axis=-1, keepdims=True) * ROUTED_SCALE
    hist = jnp.sum(sel.astype(F32), axis=0, keepdims=True)
    cnt_ref[...] = jnp.broadcast_to(hist, cnt_ref.shape).astype(jnp.int32)


def _router(h, w_router_l, bias_l):
    m, d = h.shape
    tm = min(m, 512)
    out = pl.BlockSpec((tm, LANES), lambda i: (i, 0))
    idx, wt, cnt = pl.pallas_call(
        _router_kernel,
        out_shape=[jax.ShapeDtypeStruct((m, LANES), jnp.int32), jax.ShapeDtypeStruct((m, LANES), F32),
                   jax.ShapeDtypeStruct((m // tm, SUBLANES, N_EXPERTS), jnp.int32)],
        grid=(m // tm,),
        in_specs=[pl.BlockSpec((tm, d), lambda i: (i, 0)),
                  pl.BlockSpec((d, N_EXPERTS), lambda i: (0, 0)),
                  pl.BlockSpec((1, N_EXPERTS), lambda i: (0, 0))],
        out_specs=[out, out, pl.BlockSpec((None, SUBLANES, N_EXPERTS), lambda i: (i, 0, 0))],
        compiler_params=_params(1),
        name="router",
    )(h, w_router_l, bias_l.reshape(1, N_EXPERTS))
    return idx, wt, jnp.sum(cnt[:, 0, :], axis=0)


def _swiglu_bf16(x, wg, wu, wd):
    g = jnp.dot(x, wg, preferred_element_type=F32)
    u = jnp.dot(x, wu, preferred_element_type=F32)
    return jnp.dot((_silu(g) * u).astype(BF16), wd, preferred_element_type=F32)


def _shared_kernel(h_ref, wg_ref, wu_ref, wd_ref, o_ref, wgb, wub, wdb):
    @pl.when(pl.program_id(0) == 0)
    def _():
        wgb[...] = wg_ref[...].astype(BF16)
        wub[...] = wu_ref[...].astype(BF16)
        wdb[...] = wd_ref[...].astype(BF16)

    o_ref[...] = _swiglu_bf16(h_ref[...].astype(BF16), wgb[...], wub[...], wdb[...])


def _shared_expert(h, wg, wu, wd, layer):
    m, d = h.shape
    tm = min(m, 512)
    return pl.pallas_call(
        _shared_kernel,
        out_shape=jax.ShapeDtypeStruct((m, d), F32),
        grid=(m // tm,),
        in_specs=[pl.BlockSpec((tm, d), lambda i: (i, 0)),
                  pl.BlockSpec((None, d, D_SHARED), lambda i: (layer, 0, 0)),
                  pl.BlockSpec((None, d, D_SHARED), lambda i: (layer, 0, 0)),
                  pl.BlockSpec((None, D_SHARED, d), lambda i: (layer, 0, 0))],
        out_specs=pl.BlockSpec((tm, d), lambda i: (i, 0)),
        scratch_shapes=[pltpu.VMEM((d, D_SHARED), BF16), pltpu.VMEM((d, D_SHARED), BF16),
                        pltpu.VMEM((D_SHARED, d), BF16)],
        compiler_params=_params(1),
        name="shared_expert",
    )(h, wg, wu, wd)


def _experts_kernel(blk_e, src, dst, n_used, h_hbm, wrow_ref, wg_ref, wu_ref, wd_ref, y_hbm,
                    xs, ys, wgb, wub, wdb, gsem, ssem):
    i = pl.program_id(0)
    n_act = n_used[0]
    last = pl.num_programs(0) - 1
    groups = MOE_ROWS // SUBLANES

    def gather(blk, slot, r, src_row):
        return pltpu.make_async_copy(h_hbm.at[src_row], xs.at[slot, r // SUBLANES, :, r % SUBLANES, :],
                                     gsem.at[slot])

    def scatter(blk, slot, r, dst_row):
        return pltpu.make_async_copy(ys.at[slot, r // SUBLANES, :, r % SUBLANES, :], y_hbm.at[dst_row],
                                     ssem.at[slot])

    def start_gathers(blk, slot):
        for r in range(MOE_ROWS):
            gather(blk, slot, r, src[blk * MOE_ROWS + r]).start()

    def wait_gathers(slot):
        for r in range(MOE_ROWS):
            gather(0, slot, r, 0).wait()

    def start_scatters(blk, slot):
        for r in range(MOE_ROWS):
            scatter(blk, slot, r, dst[blk * MOE_ROWS + r]).start(priority=r % 2)

    def wait_scatters(slot):
        for r in range(MOE_ROWS):
            scatter(0, slot, r, 0).wait()

    @pl.when(i < n_act)
    def _():
        slot = i % 2
        nxt = jnp.minimum(i + 1, last)

        @pl.when(i == 0)
        def _():
            start_gathers(0, 0)

        @pl.when((i == 0) | (blk_e[i] != blk_e[jnp.maximum(i - 1, 0)]))
        def _():
            wgb[...] = wg_ref[...].astype(BF16)
            wub[...] = wu_ref[...].astype(BF16)
            wdb[...] = wd_ref[...].astype(BF16)

        wait_gathers(slot)

        @pl.when(i >= 2)
        def _():
            wait_scatters(slot)

        start_gathers(nxt, 1 - slot)
        x = jnp.concatenate([xs[slot, :, t].reshape(MOE_ROWS, LANES) for t in range(ROW_TILES)], axis=1)
        y = _swiglu_bf16(x.astype(BF16), wgb[...], wub[...], wdb[...]) * wrow_ref[...]
        for t in range(ROW_TILES):
            ys[slot, :, t] = y[:, t * LANES:(t + 1) * LANES].reshape(groups, SUBLANES, LANES)
        start_scatters(i, slot)

        @pl.when(i == n_act - 1)
        def _():
            wait_gathers(1 - slot)
            wait_scatters(slot)

            @pl.when(i >= 1)
            def _():
                wait_scatters(1 - slot)


def _routed_experts(h_rows, top_idx, top_w, counts, wg, wu, wd, layer, n_out_rows):
    m = h_rows.shape[0]
    a = m * TOP_K
    n_blk = -(-(a + N_EXPERTS * (MOE_ROWS - 1)) // MOE_ROWS)
    n_rows = n_blk * MOE_ROWS
    flat_e = top_idx.reshape(a)
    order = jnp.argsort(flat_e).astype(jnp.int32)
    start = jnp.cumsum(counts) - counts
    padded = (counts + MOE_ROWS - 1) // MOE_ROWS * MOE_ROWS
    ends = jnp.cumsum(padded)
    pstart = ends - padded
    blk_e = jnp.minimum(jnp.searchsorted(ends, jnp.arange(n_blk, dtype=jnp.int32) * MOE_ROWS, side='right'),
                        N_EXPERTS - 1).astype(jnp.int32)
    n_used = (ends[-1:] // MOE_ROWS).astype(jnp.int32)
    pos = jnp.arange(n_rows, dtype=jnp.int32)
    per_row = lambda per_blk: jnp.broadcast_to(per_blk[:, None], (n_blk, MOE_ROWS)).reshape(n_rows)
    rank = pos - per_row(pstart[blk_e])
    valid = rank < per_row(counts[blk_e])
    asg = order[jnp.clip(per_row(start[blk_e]) + rank, 0, a - 1)]
    src = jnp.where(valid, asg // TOP_K, 0)
    dst = jnp.where(valid, asg, a + pos % (2 * MOE_ROWS))
    wrow = jnp.where(valid, top_w.reshape(a)[asg], 0.0).reshape(n_rows, 1)

    d = D_MODEL
    groups = MOE_ROWS // SUBLANES
    wspec = lambda i, be, s, t, n: (layer, be[i], 0, 0)
    grid_spec = pltpu.PrefetchScalarGridSpec(
        num_scalar_prefetch=4,
        grid=(n_blk,),
        in_specs=[pl.BlockSpec(memory_space=pl.ANY),
                  pl.BlockSpec((MOE_ROWS, 1), lambda i, be, s, t, n: (i, 0)),
                  pl.BlockSpec((None, None, d, D_EXPERT), wspec),
                  pl.BlockSpec((None, None, d, D_EXPERT), wspec),
                  pl.BlockSpec((None, None, D_EXPERT, d), wspec)],
        out_specs=pl.BlockSpec(memory_space=pl.ANY),
        scratch_shapes=[pltpu.VMEM((2, groups, ROW_TILES, SUBLANES, LANES), F32),
                        pltpu.VMEM((2, groups, ROW_TILES, SUBLANES, LANES), F32),
                        pltpu.VMEM((d, D_EXPERT), BF16), pltpu.VMEM((d, D_EXPERT), BF16),
                        pltpu.VMEM((D_EXPERT, d), BF16),
                        pltpu.SemaphoreType.DMA((2,)), pltpu.SemaphoreType.DMA((2,))],
    )
    return pl.pallas_call(
        _experts_kernel,
        out_shape=jax.ShapeDtypeStruct((n_out_rows, ROW_TILES, LANES), F32),
        grid_spec=grid_spec,
        compiler_params=_params(1),
        name="routed_experts",
    )(blk_e, src, dst, n_used, h_rows, wrow, wg, wu, wd)


def _combine_kernel(alpha, has_mod, y_ref, sh_ref, x_ref, gt_ref, g_ref, b_ref, *rest):
    if has_mod:
        sc_ref, shift_ref, xo_ref, ho_ref, slab_s, z_s = rest
    else:
        xo_ref, slab_s, z_s = rest
    routed = y_ref[:, 0]
    for k in range(1, TOP_K):
        routed = routed + y_ref[:, k]
    slab_s[...] = routed
    for t in range(ROW_TILES):
        cols = slice(t * LANES, (t + 1) * LANES)
        z_s[:, cols] = sh_ref[:, cols] + slab_s[:, t, :]
    z = alpha * x_ref[...] + gt_ref[...] * z_s[...]
    mu = jnp.mean(z, axis=-1, keepdims=True)
    dz = z - mu
    var = jnp.mean(dz * dz, axis=-1, keepdims=True)
    xn = dz * lax.rsqrt(var + LN_EPS) * g_ref[...] + b_ref[...]
    xo_ref[...] = xn
    if has_mod:
        ho_ref[...] = (xn * (1.0 + sc_ref[...]) + shift_ref[...]).astype(ho_ref.dtype)


def _combine_ln(alpha, y_rows, shared, x, gate, ln_g, ln_b, tok0, sc=None, sh=None):
    g, r, d = x.shape
    tr = min(r, 128)
    per_row = gate.shape[1] == r and r > 1
    has_mod = sc is not None
    nt = r // tr
    t0 = tok0 // tr
    tile = pl.BlockSpec((None, tr, d), lambda g, i: (g, i, 0))
    flat = pl.BlockSpec((tr, d), lambda g, i: (g * nt + i, 0))
    vec = pl.BlockSpec((1, d), lambda g, i: (0, 0))
    in_specs = [pl.BlockSpec((tr, TOP_K, ROW_TILES, LANES), lambda g, i: (t0 + g * nt + i, 0, 0, 0)),
                flat, tile, _row_spec(tr, d, per_row), vec, vec]
    args = [y_rows, shared, x, gate, ln_g.reshape(1, d), ln_b.reshape(1, d)]
    out_shape = [jax.ShapeDtypeStruct((g, r, d), F32)]
    out_specs = [tile]
    if has_mod:
        in_specs += [_row_spec(tr, d, per_row), _row_spec(tr, d, per_row)]
        args += [sc, sh]
        out_shape.append(jax.ShapeDtypeStruct((g, r, d), BF16))
        out_specs.append(tile)
    res = pl.pallas_call(
        functools.partial(_combine_kernel, alpha, has_mod),
        out_shape=out_shape,
        grid=(g, nt),
        in_specs=in_specs,
        out_specs=out_specs,
        scratch_shapes=[pltpu.VMEM((tr, ROW_TILES, LANES), F32), pltpu.VMEM((tr, d), F32)],
        compiler_params=_params(2),
        name="moe_combine_ln",
    )(*args)
    return res if has_mod else (res[0], None)


def kernel(x_prompt, x_sample, cache_k, cache_v, state_ret, page_table, c_prompt, c_sample, w_ada, b_ada, w_in, w_attn_br, ret_norm_g, w_ret_br, w_o, ln_g, ln_b, w_router, router_bias, w_exp_gate, w_exp_up, w_exp_down, w_sh_gate, w_sh_up, w_sh_down):
    bsz, seq, d = x_prompt.shape
    s_cnt = x_sample.shape[0]
    depth = w_in.shape[0]
    alpha = (2.0 * depth) ** 0.25
    page_rows = cache_k.shape[2]
    past = page_table.shape[1] * page_rows
    m_p = bsz * seq
    m_all = m_p + s_cnt
    y_tokens = m_all + -(-2 * MOE_ROWS // TOP_K)

    n_c = bsz + s_cnt
    r_c = -(-n_c // SUBLANES) * SUBLANES
    c_all = jnp.concatenate([c_prompt, c_sample, jnp.zeros((r_c - n_c, d), F32)], axis=0)
    mod = _adaln(c_all, w_ada, b_ada)
    mod_p = mod[:, :bsz].reshape(depth, bsz, N_MOD, 1, d)
    mod_s = mod[:, bsz:n_c].reshape(depth, 1, s_cnt, N_MOD, d)

    def mods(layer):
        mp = [mod_p[layer, :, j] for j in range(N_MOD)]
        ms = [mod_s[layer, :, :, j] for j in range(N_MOD)]
        return mp, ms

    pt_flat = page_table.reshape(-1)

    xp = x_prompt
    xs = x_sample.reshape(1, s_cnt, d)
    mp, ms = mods(0)
    hp = _modulate(xp, mp[1], mp[0], BF16)
    hs = _modulate(xs, ms[1], ms[0], BF16)

    kp, vp, sp, kd, vd = [], [], [], [], []
    state_out = None
    for layer in range(depth):
        mp, ms = mods(layer)
        pp = _mm(hp.reshape(m_p, d), w_in, layer, tn=1024, out_dtype=F32)
        o_a = _moba_prompt(pp, bsz, seq)
        o_r, s_new = _ret_prompt(pp, bsz, seq, ret_norm_g[layer])
        t1 = _mm(o_a, w_attn_br, layer, tn=512, out_dtype=F32, gate=pp, gate_col=OFF_BA)
        mixed = _mm(o_r, w_ret_br, layer, tn=512, out_dtype=BF16, gate=pp, gate_col=OFF_BR, add=t1)
        y_p = _mm(mixed, w_o, layer, tn=512, out_dtype=F32).reshape(bsz, seq, d)
        xp, h2p, hr_p = _post_ln(alpha, xp, y_p, mp[2], ln_g[layer, 0], ln_b[layer, 0], mp[4], mp[3])
        kp.append(pp[:, OFF_KA:OFF_VA].reshape(bsz, seq, KV_HEADS, HEAD_DIM))
        vp.append(pp[:, OFF_VA:OFF_QR].reshape(bsz, seq, KV_HEADS, HEAD_DIM))
        sp.append(s_new)

        ps = _mm(hs.reshape(s_cnt, d), w_in, layer, tn=1024, out_dtype=F32)
        k_new = ps[:, OFF_KA:OFF_VA].reshape(s_cnt, KV_HEADS, HEAD_DIM)
        v_new = ps[:, OFF_VA:OFF_QR].reshape(s_cnt, KV_HEADS, HEAD_DIM)
        o_as = _moba_sample(ps[:, OFF_QA:OFF_KA].reshape(s_cnt, N_HEADS, HEAD_DIM),
                            jnp.repeat(k_new, Q_PER_KV, axis=1), jnp.repeat(v_new, Q_PER_KV, axis=1),
                            cache_k, cache_v, pt_flat, layer)
        q_t = ps[:, OFF_QR:OFF_KR].reshape(s_cnt, RET_HEADS, RET_DK).transpose(0, 2, 1)
        k_t = ps[:, OFF_KR:OFF_VR].reshape(s_cnt, RET_HEADS, RET_DK).transpose(0, 2, 1)
        o_rs, state_out = _ret_sample(q_t, k_t, ps[:, OFF_VR:OFF_GR].reshape(s_cnt, 1, RET_V_W),
                                      ps[:, OFF_GR:OFF_BA].reshape(s_cnt, 1, RET_V_W),
                                      state_ret, layer, past, ret_norm_g[layer], state_out)
        t1s = _mm(o_as.reshape(s_cnt, ATTN_Q_W).astype(BF16), w_attn_br, layer, tn=512, out_dtype=F32,
                  gate=ps, gate_col=OFF_BA)
        mixed_s = _mm(o_rs.reshape(s_cnt, RET_V_W).astype(BF16), w_ret_br, layer, tn=512, out_dtype=BF16,
                      gate=ps, gate_col=OFF_BR, add=t1s)
        y_s = _mm(mixed_s, w_o, layer, tn=512, out_dtype=F32).reshape(1, s_cnt, d)
        xs, h2s, hr_s = _post_ln(alpha, xs, y_s, ms[2], ln_g[layer, 0], ln_b[layer, 0], ms[4], ms[3])
        kd.append(k_new.reshape(s_cnt, 1, KV_HEADS, HEAD_DIM))
        vd.append(v_new.reshape(s_cnt, 1, KV_HEADS, HEAD_DIM))

        h2p = h2p.reshape(m_p, d)
        h2s = h2s.reshape(s_cnt, d)
        idx_p, wt_p, cnt_p = _router(h2p, w_router[layer], router_bias[layer])
        idx_s, wt_s, cnt_s = _router(h2s, w_router[layer], router_bias[layer])
        sh_p = _shared_expert(h2p, w_sh_gate, w_sh_up, w_sh_down, layer)
        sh_s = _shared_expert(h2s, w_sh_gate, w_sh_up, w_sh_down, layer)
        h_rows = jnp.concatenate([hr_p, hr_s], axis=0)
        top_idx = jnp.concatenate([idx_p[:, :TOP_K], idx_s[:, :TOP_K]], axis=0)
        top_w = jnp.concatenate([wt_p[:, :TOP_K], wt_s[:, :TOP_K]], axis=0)
        y_rows = _routed_experts(h_rows, top_idx, top_w, cnt_p + cnt_s, w_exp_gate, w_exp_up, w_exp_down, layer,
                                 y_tokens * TOP_K)
        y_rows = y_rows.reshape(y_tokens, TOP_K, ROW_TILES, LANES)
        if layer + 1 < depth:
            mp_n, ms_n = mods(layer + 1)
            nxt_p, nxt_s = (mp_n[1], mp_n[0]), (ms_n[1], ms_n[0])
        else:
            nxt_p = nxt_s = (None, None)
        xp, hp = _combine_ln(alpha, y_rows, sh_p, xp, mp[5], ln_g[layer, 1], ln_b[layer, 1], 0, *nxt_p)
        xs, hs = _combine_ln(alpha, y_rows, sh_s, xs, ms[5], ln_g[layer, 1], ln_b[layer, 1], m_p, *nxt_s)

    return (xp, xs.reshape(s_cnt, 1, d), jnp.stack(kp), jnp.stack(vp), jnp.stack(sp).astype(state_ret.dtype),
            jnp.stack(kd), jnp.stack(vd), state_out)
```
